```python
import math
import jax, jax.numpy as jnp
from jax import lax
import numpy as np

D_MODEL = 2048
BATCH = 1
SEQ = 8192
DEPTH = 2
DEC_BATCH = 128
DEC_SEQ = 8
PAST_LEN = 8192
PAGE_SIZE = 128

N_HEADS = 32
HEAD_DIM = 64
KV_A = 4
KV_B = 4
G_A = N_HEADS // KV_A
G_B = N_HEADS // KV_B
WINDOW = 128
NSA_WINDOW = 128
BLK = 64
N_SEL = 16
CMP_HIDDEN = 256
SEL_Q_CHUNK = 128
N_EXPERTS = 32
TOP_K = 4
D_FF = D_MODEL
SWIGLU_LIMIT = 7.0
SWIGLU_ALPHA = 1.702
REL_BUCKETS = 32
REL_MAX_DIST = 128
N_A = DEPTH // 2
N_B = DEPTH - N_A
EPS = 1e-6
NEG = -1e30
TINY = 1e-20
FORCE_SCORE = 1e4

kernel_name = 'yoco_swa_sink_nsa_moe_step'


def rmsnorm(x, g):
    xf = x.astype(jnp.float32)
    y = xf * lax.rsqrt(jnp.mean(xf * xf, axis=-1, keepdims=True) + EPS)
    return (y * g.astype(jnp.float32)).astype(x.dtype)


def adaln(c, w, b, n):
    return jnp.split(jax.nn.silu(c) @ w + b, n, axis=-1)


def modulate(x, g, shift, scale):
    return rmsnorm(x, g) * (1 + scale[:, None]) + shift[:, None]


def rel_bucket(dist):
    n = jnp.maximum(dist, 0)
    max_exact = REL_BUCKETS // 2
    nf = jnp.maximum(n, 1).astype(jnp.float32)
    large = max_exact + (jnp.log(nf / max_exact) / math.log(REL_MAX_DIST / max_exact)
                         * (REL_BUCKETS - max_exact)).astype(jnp.int32)
    return jnp.where(n < max_exact, n, jnp.minimum(large, REL_BUCKETS - 1))


def rel_bias(dist, table):
    return table[rel_bucket(dist)].astype(jnp.float32)


def gqa_attend(q, k, v, q_pos, k_pos, table, window, sink=None):
    kvh, g, hd = q.shape[-3:]
    dist = q_pos[..., :, None] - k_pos[..., None, :]
    mask = (dist >= 0) & (dist < window) & (k_pos[..., None, :] >= 0)
    bias = rel_bias(dist, table)
    bias = jnp.moveaxis(bias.reshape(bias.shape[:-1] + (kvh, g)), (-2, -1), (-4, -3))
    logits = jnp.einsum('...qkgd,...skd->...kgqs', q, k, preferred_element_type=jnp.float32) * hd ** -0.5 + bias
    logits = jnp.where(mask[..., None, None, :, :], logits, NEG)
    if sink is None:
        p = jax.nn.softmax(logits, axis=-1)
    else:
        s = sink.astype(jnp.float32).reshape(kvh, g, 1, 1)
        m = jnp.maximum(logits.max(-1, keepdims=True), s)
        e = jnp.exp(logits - m)
        p = e / (e.sum(-1, keepdims=True) + jnp.exp(s - m))
    return jnp.einsum('...kgqs,...skd->...qkgd', p.astype(v.dtype), v)


def banded_attend(q, k, v, table, window, sink=None):
    B, T = q.shape[:2]
    nb = T // window

    def blocks_with_prev(a):
        ap = jnp.concatenate([jnp.zeros_like(a[:, :window]), a], axis=1)
        prev = ap[:, :T].reshape((B, nb, window) + a.shape[2:])
        cur = a.reshape((B, nb, window) + a.shape[2:])
        return jnp.concatenate([prev, cur], axis=2)

    pos = jnp.arange(T).reshape(nb, window)
    k_pos = jnp.concatenate([pos - window, pos], axis=1)
    o = gqa_attend(q.reshape((B, nb, window) + q.shape[2:]), blocks_with_prev(k), blocks_with_prev(v),
                   pos, k_pos, table, window, sink)
    return o.reshape(q.shape)


def window_step(q, k, v, buf, past_len, table, window, sink=None):
    wb, T = buf.shape[1], q.shape[1]
    kk = jnp.concatenate([buf[:, :, 0], k], axis=1)
    vv = jnp.concatenate([buf[:, :, 1], v], axis=1)
    k_pos = past_len - wb + jnp.arange(wb + T)
    q_pos = past_len + jnp.arange(T)
    return gqa_attend(q, kk, vv, q_pos, k_pos, table, window, sink)


def swa_qkv(h, w_qkv):
    B, T = h.shape[:2]
    y = h @ w_qkv
    nq, nk = N_HEADS * HEAD_DIM, KV_A * HEAD_DIM
    q = y[..., :nq].reshape(B, T, KV_A, G_A, HEAD_DIM)
    k = y[..., nq:nq + nk].reshape(B, T, KV_A, HEAD_DIM)
    v = y[..., nq + nk:].reshape(B, T, KV_A, HEAD_DIM)
    return q, k, v


def shared_kv(x, c, g, w_ada, b_ada, w_kv):
    shift, scale = adaln(c, w_ada, b_ada, 2)
    h = modulate(x, g, shift, scale)
    return (h @ w_kv).reshape(x.shape[:2] + (6, KV_B, HEAD_DIM))


def pad_blocks(rows, axis):
    widths = [(0, 0)] * rows.ndim
    widths[axis] = (0, (-rows.shape[axis]) % BLK)
    return jnp.pad(rows, widths)


def compress(blocks, pe, w1, w2):
    x = blocks + pe[:, None, :]
    x = jnp.moveaxis(x, -3, -2)
    x = x.reshape(x.shape[:-2] + (-1,))
    return jax.nn.gelu(x @ w1) @ w2


def compress_pair(rows, cmp_pe, cmp_w1, cmp_w2):
    L = rows.shape[-4]
    blocks = rows.reshape(rows.shape[:-4] + (L // BLK, BLK) + rows.shape[-3:])
    kc = compress(blocks[..., 0, :, :], cmp_pe[0], cmp_w1[0], cmp_w2[0])
    vc = compress(blocks[..., 1, :, :], cmp_pe[1], cmp_w1[1], cmp_w2[1])
    return kc, vc


def seq_rows(pool, pt, new_rows, lo):
    past = pool[pt, :, lo:lo + 2]
    past = past.reshape((-1,) + past.shape[2:])
    return pad_blocks(jnp.concatenate([past, new_rows[:, lo:lo + 2]], axis=0), 0)


def nsa_branches(q, kc, vc, ks, vs, q_pos, table):
    T, kvh, g, hd = q.shape
    nb = kc.shape[0]
    scale = hd ** -0.5
    blk = jnp.arange(nb)
    dist_c = q_pos[:, None] - (blk * BLK + (BLK - 1))[None, :]
    vis = (dist_c >= 0)[:, None, None, :]
    bias_c = jnp.moveaxis(rel_bias(dist_c, table).reshape(T, nb, kvh, g), 1, -1)
    logits = jnp.einsum('tkgd,nkd->tkgn', q, kc, preferred_element_type=jnp.float32) * scale + bias_c
    logits = jnp.where(vis, logits, NEG)
    e = jnp.where(vis, jnp.exp(logits - logits.max(-1, keepdims=True)), 0.0)
    p_cmp = e / jnp.maximum(e.sum(-1, keepdims=True), TINY)
    o_cmp = jnp.einsum('tkgn,nkd->tkgd', p_cmp.astype(vc.dtype), vc)

    cur = (q_pos // BLK)[:, None]
    forced = (blk == 0) | (blk == cur) | (blk == cur - 1)
    imp = jnp.where(forced[:, None, :], FORCE_SCORE, p_cmp.sum(axis=2))
    imp = jnp.where((blk <= cur)[:, None, :], imp, NEG)
    n_sel = min(N_SEL, nb)
    sel = lax.top_k(imp, n_sel)[1]

    ks_t = jnp.moveaxis(ks.reshape(nb, BLK, kvh, hd), 2, 0)
    vs_t = jnp.moveaxis(vs.reshape(nb, BLK, kvh, hd), 2, 0)
    kv_ix = jnp.arange(kvh)[None, :, None]
    tb = table.reshape(REL_BUCKETS, kvh, g)

    def sel_chunk(args):
        qc, pc, ic = args
        c = qc.shape[0]
        kg = ks_t[kv_ix, ic].reshape(c, kvh, n_sel * BLK, hd)
        vg = vs_t[kv_ix, ic].reshape(c, kvh, n_sel * BLK, hd)
        kpos = (ic[..., None] * BLK + jnp.arange(BLK)).reshape(c, kvh, n_sel * BLK)
        dist = pc[:, None, None] - kpos
        bias = jnp.moveaxis(tb[rel_bucket(dist), kv_ix], -1, -2).astype(jnp.float32)
        lg = jnp.einsum('ckgd,cksd->ckgs', qc, kg, preferred_element_type=jnp.float32) * scale + bias
        lg = jnp.where((dist >= 0)[:, :, None, :], lg, NEG)
        p = jax.nn.softmax(lg, axis=-1)
        return jnp.einsum('ckgs,cksd->ckgd', p.astype(vg.dtype), vg)

    c = min(SEL_Q_CHUNK, T)
    n_c = -(-T // c)
    pad = n_c * c - T

    def padq(a):
        return jnp.concatenate([a, jnp.repeat(a[-1:], pad, axis=0)], axis=0)

    o_slc = lax.map(sel_chunk, (padq(q).reshape((n_c, c) + q.shape[1:]),
                                padq(q_pos).reshape(n_c, c),
                                padq(sel).reshape(n_c, c, kvh, n_sel)))
    o_slc = o_slc.reshape((n_c * c,) + q.shape[1:])[:T]
    return o_cmp, o_slc


def nsa_merge(h, o_cmp, o_slc, o_win, w_gate, w_o):
    B, T = h.shape[:2]
    g = jax.nn.sigmoid(h @ w_gate).reshape(B, T, 3, KV_B, G_B, 1)
    o = g[:, :, 0] * o_cmp + g[:, :, 1] * o_slc + g[:, :, 2] * o_win
    return o.reshape(B, T, -1) @ w_o


def moe(h, router_w, router_b, w_up, b_up, w_down, b_down):
    shp = h.shape
    t = h.reshape(-1, shp[-1])
    logits = (t @ router_w + router_b).astype(jnp.float32)
    top_v, top_i = lax.top_k(logits, TOP_K)
    wts = jax.nn.softmax(top_v, axis=-1)
    comb = jnp.sum(jax.nn.one_hot(top_i, N_EXPERTS, dtype=jnp.float32) * wts[..., None], axis=1).astype(t.dtype)

    def expert(acc, p):
        wu, bu, wd, bd, ce = p
        u = t @ wu + bu
        glu = jnp.minimum(u[:, :D_FF], SWIGLU_LIMIT)
        lin = jnp.clip(u[:, D_FF:], -SWIGLU_LIMIT, SWIGLU_LIMIT)
        act = glu * jax.nn.sigmoid(SWIGLU_ALPHA * glu) * (lin + 1)
        return acc + ce[:, None] * (act @ wd + bd), None

    out, _ = lax.scan(expert, jnp.zeros_like(t), (w_up, b_up, w_down, b_down, comb.T))
    return out.reshape(shp)


def setup_inputs(seed: int = 0) -> dict:
    key = jax.random.key(seed)
    ks = jax.random.split(key, 40)
    f32 = jnp.float32

    def nrm(k, shape, s):
        return jax.random.normal(k, shape, f32) * s

    D, hd = D_MODEL, HEAD_DIM
    n_pages = PAST_LEN // PAGE_SIZE
    n_used = DEC_BATCH * n_pages
    n_pool = n_used + n_used // 4
    wb_a = min(WINDOW, PAST_LEN)
    wb_b = min(NSA_WINDOW, PAST_LEN)
    page_table = jax.random.permutation(ks[0], n_pool)[:n_used].reshape(DEC_BATCH, n_pages).astype(jnp.int32)
    return {
        'x_prompt': nrm(ks[1], (BATCH, SEQ, D), 1.0),
        'x_sample': nrm(ks[2], (DEC_BATCH, DEC_SEQ, D), 1.0),
        'state_swa': nrm(ks[3], (N_A, DEC_BATCH, wb_a, 2, KV_A, hd), 1.0),
        'cache_nsa': nrm(ks[4], (n_pool, PAGE_SIZE, 4, KV_B, hd), 1.0),
        'state_nsa_win': nrm(ks[5], (DEC_BATCH, wb_b, 2, KV_B, hd), 1.0),
        'page_table': page_table,
        'c_prompt': nrm(ks[6], (BATCH, D), 1.0),
        'c_sample': nrm(ks[7], (DEC_BATCH, D), 1.0),
        'rel_table': nrm(ks[8], (REL_BUCKETS, N_HEADS), 0.5),
        'w_qkv_a': nrm(ks[9], (N_A, D, (N_HEADS + 2 * KV_A) * hd), D ** -0.5),
        'sink_a': nrm(ks[10], (N_A, N_HEADS), 0.5),
        'w_o_a': nrm(ks[11], (N_A, N_HEADS * hd, D), (N_HEADS * hd) ** -0.5),
        'w_q_b': nrm(ks[12], (N_B, D, N_HEADS * hd), D ** -0.5),
        'w_gate_b': nrm(ks[13], (N_B, D, 3 * N_HEADS), D ** -0.5),
        'w_o_b': nrm(ks[14], (N_B, N_HEADS * hd, D), (N_HEADS * hd) ** -0.5),
        'w_kv_shared': nrm(ks[15], (D, 6 * KV_B * hd), D ** -0.5),
        'cmp_pe': nrm(ks[16], (2, BLK, hd), 0.5),
        'cmp_w1': nrm(ks[17], (2, BLK * hd, CMP_HIDDEN), (BLK * hd) ** -0.5),
        'cmp_w2': nrm(ks[18], (2, CMP_HIDDEN, hd), CMP_HIDDEN ** -0.5),
        'router_w': nrm(ks[19], (DEPTH, D, N_EXPERTS), D ** -0.5),
        'router_b': nrm(ks[20], (DEPTH, N_EXPERTS), 0.01),
        'w_up': nrm(ks[21], (DEPTH, N_EXPERTS, D, 2 * D_FF), D ** -0.5),
        'b_up': nrm(ks[22], (DEPTH, N_EXPERTS, 2 * D_FF), 0.02),
        'w_down': nrm(ks[23], (DEPTH, N_EXPERTS, D_FF, D), D_FF ** -0.5),
        'b_down': nrm(ks[24], (DEPTH, N_EXPERTS, D), 0.02),
        'norm_g': 1.0 + nrm(ks[25], (DEPTH, 2, D), 0.02),
        'ada_w': nrm(ks[26], (DEPTH, 2, D, 3 * D), 0.5 * D ** -0.5),
        'ada_b': nrm(ks[27], (DEPTH, 2, 3 * D), 0.02),
        'kv_norm_g': 1.0 + nrm(ks[28], (D,), 0.02),
        'ada_kv_w': nrm(ks[29], (D, 2 * D), 0.5 * D ** -0.5),
        'ada_kv_b': nrm(ks[30], (2 * D,), 0.02),
        'final_norm_g': 1.0 + nrm(ks[31], (D,), 0.02),
    }


def reference(x_prompt, x_sample, state_swa, cache_nsa, state_nsa_win, page_table, c_prompt, c_sample,
              rel_table, w_qkv_a, sink_a, w_o_a, w_q_b, w_gate_b, w_o_b, w_kv_shared, cmp_pe, cmp_w1, cmp_w2,
              router_w, router_b, w_up, b_up, w_down, b_down, norm_g, ada_w, ada_b,
              kv_norm_g, ada_kv_w, ada_kv_b, final_norm_g):
    past_len = page_table.shape[1] * cache_nsa.shape[1]
    B, T_p = x_prompt.shape[:2]
    DB, T_s = x_sample.shape[:2]
    wb_a = state_swa.shape[2]
    wb_b = state_nsa_win.shape[1]
    pos_p = jnp.arange(T_p)
    pos_s = past_len + jnp.arange(T_s)
    xp, xs = x_prompt, x_sample
    swa_p, swa_s = [], []
    for layer in range(DEPTH):
        if layer == N_A:
            kv_p = shared_kv(xp, c_prompt, kv_norm_g, ada_kv_w, ada_kv_b, w_kv_shared)
            kv_s = shared_kv(xs, c_sample, kv_norm_g, ada_kv_w, ada_kv_b, w_kv_shared)
            rows_p = pad_blocks(kv_p[:, :, :4], 1)
            kc_p, vc_p = compress_pair(rows_p[:, :, 0:2], cmp_pe, cmp_w1, cmp_w2)
            kc_s, vc_s = lax.map(
                lambda a: compress_pair(seq_rows(cache_nsa, a[0], a[1], 0), cmp_pe, cmp_w1, cmp_w2),
                (page_table, kv_s[:, :, :4]))
            nsa_win_p = kv_p[:, -wb_b:, 4:6]
            nsa_win_s = jnp.concatenate([state_nsa_win, kv_s[:, :, 4:6]], axis=1)[:, -wb_b:]

        sh_p, sc_p, gt_p = adaln(c_prompt, ada_w[layer, 0], ada_b[layer, 0], 3)
        sh_s, sc_s, gt_s = adaln(c_sample, ada_w[layer, 0], ada_b[layer, 0], 3)
        hp = modulate(xp, norm_g[layer, 0], sh_p, sc_p)
        hs = modulate(xs, norm_g[layer, 0], sh_s, sc_s)
        if layer < N_A:
            qp, kp, vp = swa_qkv(hp, w_qkv_a[layer])
            qs, k_s, v_s = swa_qkv(hs, w_qkv_a[layer])
            o_p = banded_attend(qp, kp, vp, rel_table, WINDOW, sink_a[layer])
            o_s = window_step(qs, k_s, v_s, state_swa[layer], past_len, rel_table, WINDOW, sink_a[layer])
            swa_p.append(jnp.stack([kp[:, -wb_a:], vp[:, -wb_a:]], axis=2))
            swa_s.append(jnp.concatenate([state_swa[layer], jnp.stack([k_s, v_s], axis=2)], axis=1)[:, -wb_a:])
            m_p = o_p.reshape(B, T_p, -1) @ w_o_a[layer]
            m_s = o_s.reshape(DB, T_s, -1) @ w_o_a[layer]
        else:
            j = layer - N_A
            qp = (hp @ w_q_b[j]).reshape(B, T_p, KV_B, G_B, HEAD_DIM)
            qs = (hs @ w_q_b[j]).reshape(DB, T_s, KV_B, G_B, HEAD_DIM)
            oc_p, os_p = lax.map(
                lambda a: nsa_branches(a[0], a[1], a[2], a[3][:, 2], a[3][:, 3], pos_p, rel_table),
                (qp, kc_p, vc_p, rows_p))
            ow_p = banded_attend(qp, kv_p[:, :, 4], kv_p[:, :, 5], rel_table, NSA_WINDOW)

            def sample_seq(a):
                rows = seq_rows(cache_nsa, a[1], a[2], 2)
                return nsa_branches(a[0], a[3], a[4], rows[:, 0], rows[:, 1], pos_s, rel_table)

            oc_s, os_s = lax.map(sample_seq, (qs, page_table, kv_s[:, :, :4], kc_s, vc_s))
            ow_s = window_step(qs, kv_s[:, :, 4], kv_s[:, :, 5], state_nsa_win, past_len, rel_table, NSA_WINDOW)
            m_p = nsa_merge(hp, oc_p, os_p, ow_p, w_gate_b[j], w_o_b[j])
            m_s = nsa_merge(hs, oc_s, os_s, ow_s, w_gate_b[j], w_o_b[j])
        xp = xp + gt_p[:, None] * m_p
        xs = xs + gt_s[:, None] * m_s

        sh_p, sc_p, gt_p = adaln(c_prompt, ada_w[layer, 1], ada_b[layer, 1], 3)
        sh_s, sc_s, gt_s = adaln(c_sample, ada_w[layer, 1], ada_b[layer, 1], 3)
        xp = xp + gt_p[:, None] * moe(modulate(xp, norm_g[layer, 1], sh_p, sc_p), router_w[layer], router_b[layer],
                                      w_up[layer], b_up[layer], w_down[layer], b_down[layer])
        xs = xs + gt_s[:, None] * moe(modulate(xs, norm_g[layer, 1], sh_s, sc_s), router_w[layer], router_b[layer],
                                      w_up[layer], b_up[layer], w_down[layer], b_down[layer])

    y_prompt = rmsnorm(xp, final_norm_g)
    y_sample = rmsnorm(xs, final_norm_g)
    return (y_prompt, y_sample, jnp.stack(swa_p), jnp.stack(swa_s), kv_p[:, :, :4], kv_s[:, :, :4], nsa_win_p, nsa_win_s)
```

```python
import functools
import math

import jax
import jax.numpy as jnp
from jax import lax
from jax.experimental import pallas as pl
from jax.experimental.pallas import tpu as pltpu

f32 = jnp.float32
bf16 = jnp.bfloat16
i32 = jnp.int32

N_HEADS = 32
HEAD_DIM = 64
KV = 4
G = N_HEADS // KV
WINDOW = 128
BLK = 64
N_SEL = 16
TOP_K = 4
SWIGLU_LIMIT = 7.0
SWIGLU_ALPHA = 1.702
REL_BUCKETS = 32
REL_MAX_DIST = 128
EPS = 1e-6
NEG = -1e30
TINY = 1e-20
FORCE_SCORE = 1e4
LUT_N = 256
LANES = 128
VMEM_LIMIT = 56 * 1024 * 1024


def _cparams(sem, vmem=None):
    return pltpu.CompilerParams(dimension_semantics=sem, vmem_limit_bytes=vmem)


def _col_tile(n, cap=512):
    tn = cap
    while n % tn:
        tn //= 2
    assert tn % LANES == 0
    return tn


def _dot(a, b):
    return jnp.dot(a, b, preferred_element_type=f32)


def _dot_nt(a, b):
    return lax.dot_general(a, b, (((1,), (1,)), ((), ())), preferred_element_type=f32)


def _rel_bucket(dist):
    n = jnp.maximum(dist, 0)
    max_exact = REL_BUCKETS // 2
    nf = jnp.maximum(n, 1).astype(f32)
    large = max_exact + (jnp.log(nf / max_exact) / math.log(REL_MAX_DIST / max_exact)
                         * (REL_BUCKETS - max_exact)).astype(i32)
    return jnp.where(n < max_exact, n, jnp.minimum(large, REL_BUCKETS - 1))


def _dist_bias(lut, dist, ok):
    b = lut[jnp.clip(dist, 0, LUT_N - 1)]
    b = jnp.where(ok[..., None], b, NEG)
    r, c = dist.shape
    return b.reshape(r, c, KV, G).transpose(2, 3, 0, 1).reshape(KV, G * r, c)


def _head_rows(vec, r):
    return jnp.broadcast_to(vec.reshape(KV, G, 1, 1), (KV, G, r, 1)).reshape(KV, G * r, 1).astype(f32)


def _ada_kernel(c_ref, w_ref, b_ref, o_ref):
    a = jax.nn.silu(c_ref[...]).astype(bf16)
    o_ref[...] = _dot(a, w_ref[...].astype(bf16)) + b_ref[...]


def _ada(cmat, w, b):
    nm, d, n = w.shape
    r = cmat.shape[0]
    tn = _col_tile(n)
    return pl.pallas_call(
        _ada_kernel,
        out_shape=jax.ShapeDtypeStruct((nm, r, n), f32),
        grid=(nm, n // tn),
        in_specs=[pl.BlockSpec((r, d), lambda m, j: (0, 0)),
                  pl.BlockSpec((None, d, tn), lambda m, j: (m, 0, j)),
                  pl.BlockSpec((None, 1, tn), lambda m, j: (m, 0, j))],
        out_specs=pl.BlockSpec((None, r, tn), lambda m, j: (m, 0, j)),
        compiler_params=_cparams(("arbitrary", "arbitrary")),
        name="ada_mods",
    )(cmat, w, b)


def _modulated(x_ref, g_ref, sh_ref, sc_ref):
    x = x_ref[...]
    tm, d = x.shape
    rg = sh_ref.shape[0]
    y = x * lax.rsqrt(jnp.mean(x * x, axis=-1, keepdims=True) + EPS) * g_ref[...]
    y = y.reshape(tm // rg, rg, d) * (1.0 + sc_ref[...])[None] + sh_ref[...][None]
    return y.reshape(tm, d)


def _modnorm_kernel(x_ref, g_ref, sh_ref, sc_ref, h_ref):
    h_ref[...] = _modulated(x_ref, g_ref, sh_ref, sc_ref).astype(bf16)


def _router_kernel(x_ref, g_ref, sh_ref, sc_ref, whi_ref, wlo_ref, rb_ref, h_ref, ti_ref, tw_ref):
    h = _modulated(x_ref, g_ref, sh_ref, sc_ref)
    hi = h.astype(bf16)
    lo = (h - hi.astype(f32)).astype(bf16)
    h_ref[...] = hi
    logits = _dot(hi, whi_ref[...]) + (_dot(lo, whi_ref[...]) + _dot(hi, wlo_ref[...])) + rb_ref[...]
    lane = lax.broadcasted_iota(i32, logits.shape, 1).astype(f32)
    ti = jnp.zeros(logits.shape, f32)
    tv = jnp.zeros(logits.shape, f32)
    v0 = None
    den = None
    for k in range(TOP_K):
        m = jnp.max(logits, axis=-1, keepdims=True)
        idx = jnp.min(jnp.where(logits == m, lane, float(LANES)), axis=-1, keepdims=True)
        if k == 0:
            v0 = m
            e = jnp.ones_like(m)
            den = e
        else:
            e = jnp.exp(m - v0)
            den = den + e
        ti = jnp.where(lane == k, idx, ti)
        tv = jnp.where(lane == k, e, tv)
        logits = jnp.where(lane == idx, -jnp.inf, logits)
    ti_ref[...] = ti.astype(i32)
    tw_ref[...] = tv / den


def _mod_specs(tm, d, rg, n_prompt_tiles, m_idx, cols):
    def spec(col):
        return pl.BlockSpec((None, rg, d), lambda i: (m_idx, jnp.where(i < n_prompt_tiles, 0, 1), col))
    return [spec(c) for c in cols]


def _modnorm(x, g, mods, m_idx, tp, router=None, tm=256):
    t, d = x.shape
    rg = mods.shape[1] // 2
    grid = (t // tm,)
    in_specs = [pl.BlockSpec((tm, d), lambda i: (i, 0)),
                pl.BlockSpec((1, d), lambda i: (0, 0))] + _mod_specs(tm, d, rg, tp // tm, m_idx, (0, 1))
    h_shape = jax.ShapeDtypeStruct((t, d), bf16)
    h_spec = pl.BlockSpec((tm, d), lambda i: (i, 0))
    if router is None:
        return pl.pallas_call(
            _modnorm_kernel, out_shape=h_shape, grid=grid, in_specs=in_specs, out_specs=h_spec,
            compiler_params=_cparams(("arbitrary",)), name="modnorm",
        )(x, g, mods, mods)
    whi, wlo, rb = router
    in_specs += [pl.BlockSpec((d, LANES), lambda i: (0, 0)),
                 pl.BlockSpec((d, LANES), lambda i: (0, 0)),
                 pl.BlockSpec((1, LANES), lambda i: (0, 0))]
    lane_spec = pl.BlockSpec((tm, LANES), lambda i: (i, 0))
    return pl.pallas_call(
        _router_kernel,
        out_shape=(h_shape, jax.ShapeDtypeStruct((t, LANES), i32), jax.ShapeDtypeStruct((t, LANES), f32)),
        grid=grid, in_specs=in_specs, out_specs=(h_spec, lane_spec, lane_spec),
        compiler_params=_cparams(("arbitrary",)), name="modnorm_router",
    )(x, g, mods, mods, whi, wlo, rb)


def _mm_kernel(a_ref, w_ref, o_ref):
    o_ref[...] = _dot(a_ref[...].astype(bf16), w_ref[...])


def _mm_res_kernel(a_ref, w_ref, x_ref, gt_ref, o_ref):
    acc = _dot(a_ref[...].astype(bf16), w_ref[...])
    tm, tn = acc.shape
    rg = gt_ref.shape[0]
    upd = acc.reshape(tm // rg, rg, tn) * gt_ref[...][None]
    o_ref[...] = x_ref[...] + upd.reshape(tm, tn)


def _mm(a, w, tm=256, tn=512):
    t, k = a.shape
    n = w.shape[1]
    tn = _col_tile(n, tn)
    return pl.pallas_call(
        _mm_kernel,
        out_shape=jax.ShapeDtypeStruct((t, n), f32),
        grid=(n // tn, t // tm),
        in_specs=[pl.BlockSpec((tm, k), lambda j, i: (i, 0)),
                  pl.BlockSpec((k, tn), lambda j, i: (0, j))],
        out_specs=pl.BlockSpec((tm, tn), lambda j, i: (i, j)),
        compiler_params=_cparams(("arbitrary", "arbitrary")),
        name="matmul",
    )(a, w)


def _mm_res(a, w, x, mods, m_idx, tp, tm=256, tn=512):
    t, k = a.shape
    n = w.shape[1]
    tn = _col_tile(n, tn)
    rg = mods.shape[1] // 2
    npt = tp // tm
    nj = n // tn
    return pl.pallas_call(
        _mm_res_kernel,
        out_shape=jax.ShapeDtypeStruct((t, n), f32),
        grid=(nj, t // tm),
        in_specs=[pl.BlockSpec((tm, k), lambda j, i: (i, 0)),
                  pl.BlockSpec((k, tn), lambda j, i: (0, j)),
                  pl.BlockSpec((tm, tn), lambda j, i: (i, j)),
                  pl.BlockSpec((None, rg, tn), lambda j, i: (m_idx, jnp.where(i < npt, 0, 1), 2 * nj + j))],
        out_specs=pl.BlockSpec((tm, tn), lambda j, i: (i, j)),
        compiler_params=_cparams(("arbitrary", "arbitrary")),
        name="matmul_residual",
    )(a, w, x, mods)


def _band_kernel(q_ref, kp_ref, kc_ref, vp_ref, vc_ref, b_ref, *rest, has_sink):
    if has_sink:
        s_ref, o_ref = rest
    else:
        (o_ref,) = rest
    g, tq, hd = q_ref.shape
    q = q_ref[...].reshape(g * tq, hd)
    kk = jnp.concatenate([kp_ref[...], kc_ref[...]], axis=0)
    vv = jnp.concatenate([vp_ref[...], vc_ref[...]], axis=0)
    s = _dot_nt(q, kk) * (hd ** -0.5) + b_ref[...]
    col = lax.broadcasted_iota(i32, s.shape, 1)
    first = pl.program_id(1) == 0
    s = jnp.where(jnp.logical_and(first, col < tq), NEG, s)
    m = jnp.max(s, axis=-1, keepdims=True)
    if has_sink:
        m = jnp.maximum(m, s_ref[...])
    e = jnp.exp(s - m)
    den = jnp.sum(e, axis=-1, keepdims=True)
    if has_sink:
        den = den + jnp.exp(s_ref[...] - m)
    o = _dot(e.astype(bf16), vv) / den
    o_ref[...] = o.reshape(g, tq, hd)


def _band_attn(q, k, v, bias, sink=None):
    _, g, t, hd = q.shape
    w = WINDOW
    kspec_p = pl.BlockSpec((None, w, hd), lambda h, i: (h, jnp.maximum(i - 1, 0), 0))
    kspec_c = pl.BlockSpec((None, w, hd), lambda h, i: (h, i, 0))
    in_specs = [pl.BlockSpec((None, g, w, hd), lambda h, i: (h, 0, i, 0)),
                kspec_p, kspec_c, kspec_p, kspec_c,
                pl.BlockSpec((None, g * w, 2 * w), lambda h, i: (h, 0, 0))]
    args = [q, k, k, v, v, bias]
    if sink is not None:
        in_specs.append(pl.BlockSpec((None, g * w, 1), lambda h, i: (h, 0, 0)))
        args.append(sink)
    return pl.pallas_call(
        functools.partial(_band_kernel, has_sink=sink is not None),
        out_shape=jax.ShapeDtypeStruct((KV, g, t, hd), f32),
        grid=(KV, t // w),
        in_specs=in_specs,
        out_specs=pl.BlockSpec((None, g, w, hd), lambda h, i: (h, 0, i, 0)),
        compiler_params=_cparams(("arbitrary", "arbitrary")),
        name="band_attn",
    )(*args)


def _step_kernel(q_ref, k_ref, v_ref, b_ref, *rest, has_sink):
    if has_sink:
        s_ref, o_ref = rest
    else:
        (o_ref,) = rest
    hd = q_ref.shape[-1]
    for h in range(KV):
        s = _dot_nt(q_ref[h], k_ref[h]) * (hd ** -0.5) + b_ref[h]
        m = jnp.max(s, axis=-1, keepdims=True)
        if has_sink:
            m = jnp.maximum(m, s_ref[h])
        e = jnp.exp(s - m)
        den = jnp.sum(e, axis=-1, keepdims=True)
        if has_sink:
            den = den + jnp.exp(s_ref[h] - m)
        o_ref[h] = _dot(e.astype(bf16), v_ref[h]) / den


def _step_attn(q, k, v, bias, sink=None):
    db, _, r, hd = q.shape
    l = k.shape[2]
    in_specs = [pl.BlockSpec((None, KV, r, hd), lambda b: (b, 0, 0, 0)),
                pl.BlockSpec((None, KV, l, hd), lambda b: (b, 0, 0, 0)),
                pl.BlockSpec((None, KV, l, hd), lambda b: (b, 0, 0, 0)),
                pl.BlockSpec((KV, r, l), lambda b: (0, 0, 0))]
    args = [q, k, v, bias]
    if sink is not None:
        in_specs.append(pl.BlockSpec((KV, r, 1), lambda b: (0, 0, 0)))
        args.append(sink)
    return pl.pallas_call(
        functools.partial(_step_kernel, has_sink=sink is not None),
        out_shape=jax.ShapeDtypeStruct((db, KV, r, hd), f32),
        grid=(db,),
        in_specs=in_specs,
        out_specs=pl.BlockSpec((None, KV, r, hd), lambda b: (b, 0, 0, 0)),
        compiler_params=_cparams(("arbitrary",)),
        name="step_attn",
    )(*args)


def _compress_kernel(pt_ref, pool_ref, pe_ref, w1_ref, w2_ref, o_ref, buf, sem, acc,
                     *, lane_off, pgs, nchunk):
    b = pl.program_id(0)
    c = pl.program_id(1)
    step = b * nchunk + c
    nsteps = pl.num_programs(0) * nchunk
    slot = step % 2
    nbt = 2 * pgs
    half = BLK
    width = buf.shape[-1]

    def page_copy(bb, cc, p, hb, sl):
        page = pt_ref[bb, cc * pgs + p]
        return pltpu.make_async_copy(
            pool_ref.at[page, pl.ds(hb * half, half), pl.ds(lane_off, width)],
            buf.at[sl, :, 2 * p + hb, :], sem.at[sl])

    def start(st, sl):
        bb = st // nchunk
        cc = st % nchunk

        def body(p, carry):
            page_copy(bb, cc, p, 0, sl).start()
            page_copy(bb, cc, p, 1, sl).start()
            return carry
        lax.fori_loop(0, pgs, body, 0)

    @pl.when(step == 0)
    def _():
        start(step, slot)

    @pl.when(step + 1 < nsteps)
    def _():
        start(step + 1, 1 - slot)

    def wait_body(p, carry):
        page_copy(b, c, p, 0, slot).wait()
        page_copy(b, c, p, 1, slot).wait()
        return carry
    lax.fori_loop(0, pgs, wait_body, 0)

    nch = w1_ref.shape[0]
    cw = KV * HEAD_DIM
    per = 4
    for rg in range(BLK // per):
        xs = [(buf[slot, per * rg + j] + pe_ref[per * rg + j]).astype(bf16) for j in range(per)]
        for ch in range(nch):
            rows = []
            for h in range(KV):
                lo = ch * cw + h * HEAD_DIM
                rows.append(jnp.concatenate([x[:, lo:lo + HEAD_DIM] for x in xs], axis=1))
            y = jnp.concatenate(rows, axis=0)
            part = _dot(y, w1_ref[ch, rg * per * HEAD_DIM:(rg + 1) * per * HEAD_DIM, :])
            if rg == 0:
                acc[ch] = part
            else:
                acc[ch] += part
    for ch in range(nch):
        hid = jax.nn.gelu(acc[ch]).astype(bf16)
        out = _dot(hid, w2_ref[ch])
        o_ref[ch] = out.reshape(KV, nbt, HEAD_DIM)


def _compress(pool, lane_off, pt, pe_rows, w1, w2):
    nseq, npg = pt.shape
    pgs = min(npg, 32)
    nchunk = npg // pgs
    nbt = 2 * pgs
    width = 2 * KV * HEAD_DIM
    hid = w1.shape[-1]
    kern = functools.partial(_compress_kernel, lane_off=lane_off, pgs=pgs, nchunk=nchunk)
    return pl.pallas_call(
        kern,
        out_shape=jax.ShapeDtypeStruct((nseq, 2, KV, 2 * npg, HEAD_DIM), f32),
        grid_spec=pltpu.PrefetchScalarGridSpec(
            num_scalar_prefetch=1,
            grid=(nseq, nchunk),
            in_specs=[pl.BlockSpec(memory_space=pl.ANY),
                      pl.BlockSpec((BLK, 1, width), lambda b, c, pt: (0, 0, 0)),
                      pl.BlockSpec(w1.shape, lambda b, c, pt: (0, 0, 0)),
                      pl.BlockSpec(w2.shape, lambda b, c, pt: (0, 0, 0))],
            out_specs=pl.BlockSpec((None, 2, KV, nbt, HEAD_DIM), lambda b, c, pt: (b, 0, 0, c, 0)),
            scratch_shapes=[pltpu.VMEM((2, BLK, nbt, width), f32),
                            pltpu.SemaphoreType.DMA((2,)),
                            pltpu.VMEM((2, KV * nbt, hid), f32)]),
        compiler_params=_cparams(("arbitrary", "arbitrary"), VMEM_LIMIT),
        name="compress",
    )(pt, pool, pe_rows, w1, w2)


def _cmp_core(q, kc, vc, bias, cur, g, tq):
    hd = q.shape[-1]
    nb = kc.shape[0]
    vis = bias > 0.5 * NEG
    s = _dot_nt(q, kc) * (hd ** -0.5) + bias
    m = jnp.max(s, axis=-1, keepdims=True)
    e = jnp.where(vis, jnp.exp(s - m), 0.0)
    p = e / jnp.maximum(jnp.sum(e, axis=-1, keepdims=True), TINY)
    o = _dot(p.astype(bf16), vc)
    imp = jnp.sum(p.reshape(g, tq, nb), axis=0)
    col = lax.broadcasted_iota(i32, (tq, nb), 1)
    colf = col.astype(f32)
    forced = (col == 0) | (col == cur) | (col == cur - 1)
    imp = jnp.where(forced, FORCE_SCORE, imp)
    imp = jnp.where(col <= cur, imp, NEG)
    sel = jnp.zeros((tq, nb), f32)
    for _ in range(N_SEL):
        mx = jnp.max(imp, axis=-1, keepdims=True)
        idx = jnp.min(jnp.where(imp == mx, colf, float(nb)), axis=-1, keepdims=True)
        hit = colf == idx
        sel = jnp.where(hit, 1.0, sel)
        imp = jnp.where(hit, -jnp.inf, imp)
    return o, sel


def _cmp_prompt_kernel(q_ref, kc_ref, vc_ref, bc_ref, c31_ref, o_ref, sel_ref):
    g, tq, hd = q_ref.shape
    nb = kc_ref.shape[0]
    qi = pl.program_id(1)
    rows = g * tq
    col = lax.broadcasted_iota(i32, (rows, nb), 1)
    first = (tq // BLK) * qi - 2
    nband = bc_ref.shape[0]
    bias = jnp.where(col < first, c31_ref[...], NEG)
    for jj in range(nband):
        bias = jnp.where(col == first + jj, bc_ref[jj], bias)
    t = qi * tq + lax.broadcasted_iota(i32, (tq, 1), 0)
    o, sel = _cmp_core(q_ref[...].reshape(rows, hd), kc_ref[...], vc_ref[...], bias, t // BLK, g, tq)
    o_ref[...] = o.reshape(g, tq, hd)
    sel_ref[...] = sel.astype(bf16)


def _cmp_prompt(q, kc, vc, bc, c31):
    _, g, t, hd = q.shape
    nb = kc.shape[1]
    tq = WINDOW
    nband = bc.shape[1]
    return pl.pallas_call(
        _cmp_prompt_kernel,
        out_shape=(jax.ShapeDtypeStruct((KV, g, t, hd), f32), jax.ShapeDtypeStruct((KV, t, nb), bf16)),
        grid=(KV, t // tq),
        in_specs=[pl.BlockSpec((None, g, tq, hd), lambda h, i: (h, 0, i, 0)),
                  pl.BlockSpec((None, nb, hd), lambda h, i: (h, 0, 0)),
                  pl.BlockSpec((None, nb, hd), lambda h, i: (h, 0, 0)),
                  pl.BlockSpec((None, nband, g * tq, 1), lambda h, i: (h, 0, 0, 0)),
                  pl.BlockSpec((None, g * tq, 1), lambda h, i: (h, 0, 0))],
        out_specs=(pl.BlockSpec((None, g, tq, hd), lambda h, i: (h, 0, i, 0)),
                   pl.BlockSpec((None, tq, nb), lambda h, i: (h, i, 0))),
        compiler_params=_cparams(("arbitrary", "arbitrary")),
        name="cmp_prompt",
    )(q, kc, vc, bc, c31)


def _cmp_sample_kernel(q_ref, kc_ref, vc_ref, b_ref, cur_ref, o_ref, sel_ref):
    ts = cur_ref.shape[0]
    g = q_ref.shape[1] // ts
    for h in range(KV):
        o, sel = _cmp_core(q_ref[h], kc_ref[h], vc_ref[h], b_ref[h], cur_ref[...], g, ts)
        o_ref[h] = o
        sel_ref[h] = sel.astype(bf16)


def _cmp_sample(q, kc, vc, bias, cur):
    db, _, r, hd = q.shape
    nbp = kc.shape[2]
    ts = cur.shape[0]
    return pl.pallas_call(
        _cmp_sample_kernel,
        out_shape=(jax.ShapeDtypeStruct((db, KV, r, hd), f32), jax.ShapeDtypeStruct((db, KV, ts, nbp), bf16)),
        grid=(db,),
        in_specs=[pl.BlockSpec((None, KV, r, hd), lambda b: (b, 0, 0, 0)),
                  pl.BlockSpec((None, KV, nbp, hd), lambda b: (b, 0, 0, 0)),
                  pl.BlockSpec((None, KV, nbp, hd), lambda b: (b, 0, 0, 0)),
                  pl.BlockSpec((KV, r, nbp), lambda b: (0, 0, 0)),
                  pl.BlockSpec((ts, 1), lambda b: (0, 0))],
        out_specs=(pl.BlockSpec((None, KV, r, hd), lambda b: (b, 0, 0, 0)),
                   pl.BlockSpec((None, KV, ts, nbp), lambda b: (b, 0, 0, 0))),
        compiler_params=_cparams(("arbitrary",)),
        name="cmp_sample",
    )(q, kc, vc, bias, cur)


def _expand(nb, first_block, nkeys):
    r = lax.broadcasted_iota(i32, (nb, nkeys), 0)
    c = lax.broadcasted_iota(i32, (nb, nkeys), 1)
    return jnp.where(r == first_block + c // BLK, 1.0, 0.0).astype(bf16)


def _online_update(s, v, m_ref, l_ref, acc_ref):
    m_old = m_ref[...]
    m_new = jnp.maximum(m_old, jnp.max(s, axis=-1, keepdims=True))
    alpha = jnp.exp(m_old - m_new)
    e = jnp.exp(s - m_new)
    l_ref[...] = alpha * l_ref[...] + jnp.sum(e, axis=-1, keepdims=True)
    acc_ref[...] = alpha * acc_ref[...] + _dot(e.astype(bf16), v)
    m_ref[...] = m_new


def _slc_prompt_kernel(q_ref, k_ref, v_ref, sel_ref, b_ref, c31_ref, o_ref, m_ref, l_ref, acc_ref):
    g, tq, hd = q_ref.shape
    nb = sel_ref.shape[-1]
    qi = pl.program_id(1)
    rows = g * tq
    q = q_ref[...].reshape(rows, hd)
    sel = sel_ref[...]
    scale = hd ** -0.5

    def tile(kstart, bias):
        kt = k_ref[pl.ds(kstart, tq), :]
        vt = v_ref[pl.ds(kstart, tq), :]
        s = _dot_nt(q, kt) * scale + bias
        msk = _dot(sel, _expand(nb, kstart // BLK, tq))
        s = jnp.where((msk > 0.5)[None], s.reshape(g, tq, tq), NEG).reshape(rows, tq)
        _online_update(s, vt, m_ref, l_ref, acc_ref)

    m_ref[...] = jnp.full(m_ref.shape, NEG, f32)
    l_ref[...] = jnp.zeros(l_ref.shape, f32)
    acc_ref[...] = jnp.zeros(acc_ref.shape, f32)
    tile(pl.multiple_of(qi * tq, tq), b_ref[:, tq:])

    @pl.when(qi > 0)
    def _():
        tile(pl.multiple_of((qi - 1) * tq, tq), b_ref[:, :tq])

    def far(j, carry):
        tile(pl.multiple_of(j * tq, tq), c31_ref[...])
        return carry
    lax.fori_loop(0, jnp.maximum(qi - 1, 0), far, 0)
    o_ref[...] = (acc_ref[...] / l_ref[...]).reshape(g, tq, hd)


def _slc_prompt(q, k, v, sel, bias, c31, tq):
    _, g, t, hd = q.shape
    nb = sel.shape[-1]
    rows = g * tq
    return pl.pallas_call(
        _slc_prompt_kernel,
        out_shape=jax.ShapeDtypeStruct((KV, g, t, hd), f32),
        grid=(KV, t // tq),
        in_specs=[pl.BlockSpec((None, g, tq, hd), lambda h, i: (h, 0, i, 0)),
                  pl.BlockSpec((None, t, hd), lambda h, i: (h, 0, 0)),
                  pl.BlockSpec((None, t, hd), lambda h, i: (h, 0, 0)),
                  pl.BlockSpec((None, tq, nb), lambda h, i: (h, i, 0)),
                  pl.BlockSpec((None, rows, 2 * tq), lambda h, i: (h, 0, 0)),
                  pl.BlockSpec((None, rows, 1), lambda h, i: (h, 0, 0))],
        out_specs=pl.BlockSpec((None, g, tq, hd), lambda h, i: (h, 0, i, 0)),
        scratch_shapes=[pltpu.VMEM((rows, 1), f32), pltpu.VMEM((rows, 1), f32), pltpu.VMEM((rows, hd), f32)],
        compiler_params=_cparams(("arbitrary", "arbitrary"), VMEM_LIMIT),
        name="slc_prompt",
    )(q, k, v, sel, bias, c31)


def _slc_sample_kernel(pt_ref, pool_ref, q_ref, sel_ref, kn_ref, vn_ref, bl_ref, bn_ref, c31_ref, o_ref,
                       buf, sem, m_ref, l_ref, acc_ref, *, lane_off, pgs, nchunk, tk):
    b = pl.program_id(0)
    c = pl.program_id(1)
    step = b * nchunk + c
    nsteps = pl.num_programs(0) * nchunk
    slot = step % 2
    psz = pool_ref.shape[1]
    width = buf.shape[-1]
    half = width // 2
    nbp = sel_ref.shape[-1]
    rows = q_ref.shape[0]
    scale = HEAD_DIM ** -0.5
    ntile = pgs * psz // tk
    past = nchunk * pgs * psz

    def page_copy(bb, cc, p, sl):
        page = pt_ref[bb, cc * pgs + p]
        return pltpu.make_async_copy(
            pool_ref.at[page, :, pl.ds(lane_off, width)], buf.at[sl, pl.ds(p * psz, psz), :], sem.at[sl])

    def start(st, sl):
        bb = st // nchunk
        cc = st % nchunk

        def body(p, carry):
            page_copy(bb, cc, p, sl).start()
            return carry
        lax.fori_loop(0, pgs, body, 0)

    @pl.when(step == 0)
    def _():
        start(step, slot)

    @pl.when(step + 1 < nsteps)
    def _():
        start(step + 1, 1 - slot)

    def wait_body(p, carry):
        page_copy(b, c, p, slot).wait()
        return carry
    lax.fori_loop(0, pgs, wait_body, 0)

    @pl.when(c == 0)
    def _():
        m_ref[...] = jnp.full(m_ref.shape, NEG, f32)
        l_ref[...] = jnp.zeros(l_ref.shape, f32)
        acc_ref[...] = jnp.zeros(acc_ref.shape, f32)

    q = q_ref[...]
    sel = sel_ref[...]

    def masked_update(kt, vt, bias, first_block):
        nk = kt.shape[0]
        s = _dot_nt(q, kt) * scale + bias
        msk = _dot(sel, _expand(nbp, first_block, nk))
        s = jnp.where(msk > 0.5, s, NEG)
        _online_update(s, vt, m_ref, l_ref, acc_ref)

    def tile(j, carry):
        r0 = pl.multiple_of(j * tk, tk)
        kt = buf[slot, pl.ds(r0, tk), :half].astype(bf16)
        vt = buf[slot, pl.ds(r0, tk), half:].astype(bf16)
        kstart = c * (pgs * psz) + r0
        bias = jnp.where(kstart == past - tk, bl_ref[...], c31_ref[...])
        masked_update(kt, vt, bias, kstart // BLK)
        return carry
    lax.fori_loop(0, ntile, tile, 0)

    @pl.when(c == nchunk - 1)
    def _():
        masked_update(kn_ref[...], vn_ref[...], bn_ref[...], past // BLK)
        full = acc_ref[...] / l_ref[...]
        r = rows // KV
        o_ref[...] = jnp.concatenate(
            [full[h * r:(h + 1) * r, h * HEAD_DIM:(h + 1) * HEAD_DIM] for h in range(KV)], axis=0)


def _slc_sample(pool, lane_off, pt, qbd, selr, knew, vnew, bias_last, bias_new, c31):
    db, npg = pt.shape
    psz = pool.shape[1]
    rows = qbd.shape[1]
    width = 2 * KV * HEAD_DIM
    tk = 2 * psz
    pgs = min(npg, 32)
    nchunk = npg // pgs
    nbp = selr.shape[-1]
    kern = functools.partial(_slc_sample_kernel, lane_off=lane_off, pgs=pgs, nchunk=nchunk, tk=tk)
    return pl.pallas_call(
        kern,
        out_shape=jax.ShapeDtypeStruct((db, rows, HEAD_DIM), f32),
        grid_spec=pltpu.PrefetchScalarGridSpec(
            num_scalar_prefetch=1,
            grid=(db, nchunk),
            in_specs=[pl.BlockSpec(memory_space=pl.ANY),
                      pl.BlockSpec((None, rows, KV * HEAD_DIM), lambda b, c, pt: (b, 0, 0)),
                      pl.BlockSpec((None, rows, nbp), lambda b, c, pt: (b, 0, 0)),
                      pl.BlockSpec((None, BLK, KV * HEAD_DIM), lambda b, c, pt: (b, 0, 0)),
                      pl.BlockSpec((None, BLK, KV * HEAD_DIM), lambda b, c, pt: (b, 0, 0)),
                      pl.BlockSpec((rows, tk), lambda b, c, pt: (0, 0)),
                      pl.BlockSpec((rows, BLK), lambda b, c, pt: (0, 0)),
                      pl.BlockSpec((rows, 1), lambda b, c, pt: (0, 0))],
            out_specs=pl.BlockSpec((None, rows, HEAD_DIM), lambda b, c, pt: (b, 0, 0)),
            scratch_shapes=[pltpu.VMEM((2, pgs * psz, width), f32),
                            pltpu.SemaphoreType.DMA((2,)),
                            pltpu.VMEM((rows, 1), f32), pltpu.VMEM((rows, 1), f32),
                            pltpu.VMEM((rows, KV * HEAD_DIM), f32)]),
        compiler_params=_cparams(("arbitrary", "arbitrary"), VMEM_LIMIT),
        name="slc_sample",
    )(pt, pool, qbd, selr, knew, vnew, bias_last, bias_new, c31)


def _merge_kernel(gl_ref, oc_ref, os_ref, ow_ref, ex_ref, o_ref):
    gate = jax.nn.sigmoid(gl_ref[...])
    hi = gate.astype(bf16)
    lo = (gate - hi.astype(f32)).astype(bf16)
    acc = None
    for br, ref in enumerate((oc_ref, os_ref, ow_ref)):
        ex = ex_ref[br]
        gx = _dot(hi, ex) + _dot(lo, ex)
        term = gx * ref[...]
        acc = term if acc is None else acc + term
    o_ref[...] = acc.astype(bf16)


def _merge(glog, oc, osl, ow, expand, tm=256):
    t, n = oc.shape
    return pl.pallas_call(
        _merge_kernel,
        out_shape=jax.ShapeDtypeStruct((t, n), bf16),
        grid=(t // tm,),
        in_specs=[pl.BlockSpec((tm, LANES), lambda i: (i, 0)),
                  pl.BlockSpec((tm, n), lambda i: (i, 0)),
                  pl.BlockSpec((tm, n), lambda i: (i, 0)),
                  pl.BlockSpec((tm, n), lambda i: (i, 0)),
                  pl.BlockSpec((3, LANES, n), lambda i: (0, 0, 0))],
        out_specs=pl.BlockSpec((tm, n), lambda i: (i, 0)),
        compiler_params=_cparams(("arbitrary",)),
        name="nsa_merge",
    )(glog, oc, osl, ow, expand)


def _gather_rows(idx_cur, idx_nxt, cnt_cur, cnt_nxt, src_ref, buf, sem, step, nsteps):
    slot = step % 2
    nrow = buf.shape[1]

    def row_copy(idx_ref, r, sl):
        return pltpu.make_async_copy(src_ref.at[pl.ds(idx_ref[0, r], 1)], buf.at[sl, pl.ds(r, 1)], sem.at[sl])

    def start(idx_ref, sl):
        def body(r, carry):
            row_copy(idx_ref, r, sl).start()
            return carry
        lax.fori_loop(0, nrow, body, 0)

    @pl.when(jnp.logical_and(step == 0, cnt_cur[0, 0] > 0))
    def _():
        start(idx_cur, slot)

    @pl.when(jnp.logical_and(step + 1 < nsteps, cnt_nxt[0, 0] > 0))
    def _():
        start(idx_nxt, 1 - slot)

    @pl.when(cnt_cur[0, 0] > 0)
    def _():
        def body(r, carry):
            row_copy(idx_cur, r, slot).wait()
            return carry
        lax.fori_loop(0, nrow, body, 0)
    return slot


def _permute_kernel(ic_ref, in_ref, cc_ref, cn_ref, src_ref, o_ref, buf, sem):
    step = pl.program_id(0)
    slot = _gather_rows(ic_ref, in_ref, cc_ref, cn_ref, src_ref, buf, sem, step, pl.num_programs(0))

    @pl.when(cc_ref[0, 0] > 0)
    def _():
        o_ref[...] = buf[slot]

    @pl.when(cc_ref[0, 0] == 0)
    def _():
        o_ref[...] = jnp.zeros(o_ref.shape, o_ref.dtype)


def _smem_step_specs(n_idx, nsteps):
    idx_c = pl.BlockSpec((None, 1, n_idx), lambda i: (i, 0, 0), memory_space=pltpu.SMEM)
    idx_n = pl.BlockSpec((None, 1, n_idx), lambda i: (jnp.minimum(i + 1, nsteps - 1), 0, 0),
                         memory_space=pltpu.SMEM)
    cnt_c = pl.BlockSpec((None, 1, 1), lambda i: (i, 0, 0), memory_space=pltpu.SMEM)
    cnt_n = pl.BlockSpec((None, 1, 1), lambda i: (jnp.minimum(i + 1, nsteps - 1), 0, 0),
                         memory_space=pltpu.SMEM)
    return [idx_c, idx_n, cnt_c, cnt_n]


def _permute_rows(src, idx, cnt, rows_per_step):
    nsteps = idx.shape[0]
    w = src.shape[1]
    return pl.pallas_call(
        _permute_kernel,
        out_shape=jax.ShapeDtypeStruct((nsteps * rows_per_step, w), src.dtype),
        grid=(nsteps,),
        in_specs=_smem_step_specs(rows_per_step, nsteps) + [pl.BlockSpec(memory_space=pl.ANY)],
        out_specs=pl.BlockSpec((rows_per_step, w), lambda i: (i, 0)),
        scratch_shapes=[pltpu.VMEM((2, rows_per_step, w), src.dtype), pltpu.SemaphoreType.DMA((2,))],
        compiler_params=_cparams(("arbitrary",)),
        name="moe_permute",
    )(idx, idx, cnt, cnt, src)


def _moe_kernel(te_ref, ns_ref, x_ref, wg_ref, wl_ref, bg_ref, bl_ref, wd_ref, bd_ref, o_ref,
                wg_s, wl_s, wd_s, *, sub):
    t = pl.program_id(0)
    f = pl.program_id(1)
    nsub = ns_ref[t]

    @pl.when(f == 0)
    def _():
        o_ref[...] = jnp.zeros(o_ref.shape, f32)

    @pl.when(nsub > 0)
    def _():
        wg_s[...] = wg_ref[...].astype(bf16)
        wl_s[...] = wl_ref[...].astype(bf16)
        wd_s[...] = wd_ref[...].astype(bf16)

        def body(s, carry):
            r = pl.multiple_of(s * sub, sub)
            x = x_ref[pl.ds(r, sub), :]
            glu = jnp.minimum(_dot(x, wg_s[...]) + bg_ref[...], SWIGLU_LIMIT)
            lin = jnp.clip(_dot(x, wl_s[...]) + bl_ref[...], -SWIGLU_LIMIT, SWIGLU_LIMIT)
            act = glu * jax.nn.sigmoid(SWIGLU_ALPHA * glu) * (lin + 1.0)
            y = _dot(act.astype(bf16), wd_s[...])
            first = jnp.where(f == 0, 1.0, 0.0)
            o_ref[pl.ds(r, sub), :] += y + first * bd_ref[...]
            return carry
        lax.fori_loop(0, nsub, body, 0)


def _moe_experts(hs, tile_e, tile_ns, w_up, b_up, w_down, b_down, tm, sub, fc):
    sb, d = hs.shape
    ff = w_down.shape[1]
    nt = sb // tm
    nf = ff // fc

    def fidx(t, f, ns):
        return jnp.where(ns[t] > 0, f, 0)

    return pl.pallas_call(
        functools.partial(_moe_kernel, sub=sub),
        out_shape=jax.ShapeDtypeStruct((sb, d), f32),
        grid_spec=pltpu.PrefetchScalarGridSpec(
            num_scalar_prefetch=2,
            grid=(nt, nf),
            in_specs=[pl.BlockSpec((tm, d), lambda t, f, te, ns: (t, 0)),
                      pl.BlockSpec((None, d, fc), lambda t, f, te, ns: (te[t], 0, fidx(t, f, ns))),
                      pl.BlockSpec((None, d, fc), lambda t, f, te, ns: (te[t], 0, nf + fidx(t, f, ns))),
                      pl.BlockSpec((None, 1, fc), lambda t, f, te, ns: (te[t], 0, fidx(t, f, ns))),
                      pl.BlockSpec((None, 1, fc), lambda t, f, te, ns: (te[t], 0, nf + fidx(t, f, ns))),
                      pl.BlockSpec((None, fc, d), lambda t, f, te, ns: (te[t], fidx(t, f, ns), 0)),
                      pl.BlockSpec((None, 1, d), lambda t, f, te, ns: (te[t], 0, 0))],
            out_specs=pl.BlockSpec((tm, d), lambda t, f, te, ns: (t, 0)),
            scratch_shapes=[pltpu.VMEM((d, fc), bf16), pltpu.VMEM((d, fc), bf16), pltpu.VMEM((fc, d), bf16)]),
        compiler_params=_cparams(("arbitrary", "arbitrary"), VMEM_LIMIT),
        name="moe_experts",
    )(tile_e, tile_ns, hs, w_up, w_up, b_up, b_up, w_down, b_down)


def _combine_kernel(ic_ref, in_ref, cc_ref, cn_ref, y_ref, x_ref, tw_ref, gt_ref, o_ref, buf, sem):
    step = pl.program_id(0)
    slot = _gather_rows(ic_ref, in_ref, cc_ref, cn_ref, y_ref, buf, sem, step, pl.num_programs(0))
    tt, d = x_ref.shape
    rg = gt_ref.shape[0]
    tw = tw_ref[...]
    acc = None
    for k in range(TOP_K):
        term = tw[:, k:k + 1] * buf[slot, pl.ds(k * tt, tt), :]
        acc = term if acc is None else acc + term
    if tt >= rg:
        upd = (acc.reshape(tt // rg, rg, d) * gt_ref[...][None]).reshape(tt, d)
    else:
        part = pl.program_id(0) % (rg // tt)
        upd = acc * gt_ref[pl.ds(pl.multiple_of(part * tt, tt), tt), :]
    o_ref[...] = x_ref[...] + upd


def _moe_combine(y, pos, x, topw, mods, m_idx, tp, tt):
    t, d = x.shape
    nsteps = t // tt
    rg = mods.shape[1] // 2
    npt = tp // tt
    gate_rows = max(tt, rg)
    per = gate_rows // tt
    cnt = jnp.ones((nsteps, 1, 1), i32)
    return pl.pallas_call(
        _combine_kernel,
        out_shape=jax.ShapeDtypeStruct((t, d), f32),
        grid=(nsteps,),
        in_specs=_smem_step_specs(TOP_K * tt, nsteps) + [
            pl.BlockSpec(memory_space=pl.ANY),
            pl.BlockSpec((tt, d), lambda i: (i, 0)),
            pl.BlockSpec((tt, LANES), lambda i: (i, 0)),
            pl.BlockSpec((None, rg, d), lambda i: (m_idx, jnp.where(i < npt, 0, 1), 2))],
        out_specs=pl.BlockSpec((tt, d), lambda i: (i, 0)),
        scratch_shapes=[pltpu.VMEM((2, TOP_K * tt, d), f32), pltpu.SemaphoreType.DMA((2,))],
        compiler_params=_cparams(("arbitrary",)),
        name="moe_combine",
    )(pos, pos, cnt, cnt, y, x, topw, mods)


def _moe_layer(x, g, mods, m_idx, tp, router_w, router_b, w_up, b_up, w_down, b_down):
    t, d = x.shape
    ne = router_w.shape[1]
    ff = w_down.shape[1]
    tm, sub = 1024, 256
    fc = min(256, ff)
    tt = 64

    rw = jnp.zeros((d, LANES), f32).at[:, :ne].set(router_w)
    whi = rw.astype(bf16)
    wlo = (rw - whi.astype(f32)).astype(bf16)
    rb = jnp.full((1, LANES), -jnp.inf, f32).at[0, :ne].set(router_b)
    h, topi, topw = _modnorm(x, g, mods, m_idx, tp, router=(whi, wlo, rb))

    e_flat = topi[:, :TOP_K].reshape(-1)
    npair = t * TOP_K
    order = jnp.argsort(e_flat, stable=True).astype(i32)
    counts = jnp.zeros((ne,), i32).at[e_flat].add(1)
    tiles_e = (counts + tm - 1) // tm
    tile_start = jnp.cumsum(tiles_e) - tiles_e
    grp_start = jnp.cumsum(counts) - counts
    nt = npair // tm + ne
    sb = nt * tm
    e_sorted = e_flat[order]
    rank = jnp.arange(npair, dtype=i32) - grp_start[e_sorted]
    slot_sorted = tile_start[e_sorted] * tm + rank
    src_tok = jnp.zeros((sb,), i32).at[slot_sorted].set(order // TOP_K)
    valid = jnp.zeros((sb,), i32).at[slot_sorted].set(1)
    pos = jnp.zeros((npair,), i32).at[order].set(slot_sorted)
    tile_ids = jnp.arange(nt, dtype=i32)
    tile_e = jnp.clip(jnp.searchsorted(jnp.cumsum(tiles_e), tile_ids, side="right"), 0, ne - 1).astype(i32)
    rows_in_tile = jnp.clip(counts[tile_e] - (tile_ids - tile_start[tile_e]) * tm, 0, tm)
    rows_in_tile = jnp.where(tile_ids < jnp.sum(tiles_e), rows_in_tile, 0)
    tile_ns = ((rows_in_tile + sub - 1) // sub).astype(i32)
    sub_cnt = valid.reshape(sb // sub, sub).sum(axis=1).astype(i32)

    hw = lax.bitcast_convert_type(h.reshape(t, d // 2, 2), jnp.uint32)
    hs = _permute_rows(hw, src_tok.reshape(sb // sub, 1, sub), sub_cnt.reshape(-1, 1, 1), sub)
    hs = lax.bitcast_convert_type(hs, bf16).reshape(sb, d)
    y = _moe_experts(hs, tile_e, tile_ns, w_up, b_up.reshape(ne, 1, 2 * ff), w_down,
                     b_down.reshape(ne, 1, d), tm, sub, fc)
    pos_steps = pos.reshape(t // tt, tt, TOP_K).transpose(0, 2, 1).reshape(t // tt, 1, TOP_K * tt)
    return _moe_combine(y, pos_steps, x, topw, mods, m_idx, tp, tt)


def _rms_kernel(x_ref, g_ref, o_ref):
    x = x_ref[...]
    o_ref[...] = x * lax.rsqrt(jnp.mean(x * x, axis=-1, keepdims=True) + EPS) * g_ref[...]


def _rmsnorm(x, g, tm=256):
    t, d = x.shape
    return pl.pallas_call(
        _rms_kernel,
        out_shape=jax.ShapeDtypeStruct((t, d), f32),
        grid=(t // tm,),
        in_specs=[pl.BlockSpec((tm, d), lambda i: (i, 0)), pl.BlockSpec((1, d), lambda i: (0, 0))],
        out_specs=pl.BlockSpec((tm, d), lambda i: (i, 0)),
        compiler_params=_cparams(("arbitrary",)),
        name="final_norm",
    )(x, g)


def _heads_prompt(a, tp):
    return a[:tp].reshape(tp, KV, G, HEAD_DIM).transpose(1, 2, 0, 3)


def _heads_sample(a, tp, ts, db):
    return a[tp:].reshape(ts, db, KV, G, HEAD_DIM).transpose(1, 2, 3, 0, 4).reshape(db, KV, G * ts, HEAD_DIM)


def _unheads(op, osm, tp, ts, db):
    a = op.transpose(2, 0, 1, 3).reshape(tp, N_HEADS * HEAD_DIM)
    b = osm.reshape(db, KV, G, ts, HEAD_DIM).transpose(3, 0, 1, 2, 4).reshape(ts * db, N_HEADS * HEAD_DIM)
    return jnp.concatenate([a, b], axis=0)


def _kv_prompt(a, tp):
    return a[:tp].reshape(tp, KV, HEAD_DIM).transpose(1, 0, 2)


def _kv_sample(a, tp, ts, db):
    return a[tp:].reshape(ts, db, KV, HEAD_DIM).transpose(1, 0, 2, 3)


def kernel(x_prompt, x_sample, state_swa, cache_nsa, state_nsa_win, page_table, c_prompt, c_sample,
           rel_table, w_qkv_a, sink_a, w_o_a, w_q_b, w_gate_b, w_o_b, w_kv_shared, cmp_pe, cmp_w1, cmp_w2,
           router_w, router_b, w_up, b_up, w_down, b_down, norm_g, ada_w, ada_b,
           kv_norm_g, ada_kv_w, ada_kv_b, final_norm_g):
    bsz, tp, d = x_prompt.shape
    db, ts, _ = x_sample.shape
    n_a = w_qkv_a.shape[0]
    depth = norm_g.shape[0]
    npg = page_table.shape[1]
    psz = cache_nsa.shape[1]
    past = npg * psz
    wb = state_swa.shape[2]
    nq = N_HEADS * HEAD_DIM
    nk = KV * HEAD_DIM
    assert bsz == 1 and wb == WINDOW and state_nsa_win.shape[1] == WINDOW
    assert psz == 2 * BLK and past % BLK == 0 and ts <= BLK and tp % 256 == 0
    t = tp + ts * db

    x = jnp.concatenate([x_prompt[0], x_sample.transpose(1, 0, 2).reshape(ts * db, d)], axis=0)
    cmat = jnp.concatenate([jnp.broadcast_to(c_prompt, (db, d)), c_sample], axis=0)
    mods = _ada(cmat, ada_w.reshape(depth * 2, d, 3 * d), ada_b.reshape(depth * 2, 1, 3 * d))
    mods_kv = _ada(cmat, ada_kv_w[None], ada_kv_b[None, None])

    lut = rel_table[_rel_bucket(jnp.arange(LUT_N))].astype(f32)
    c31 = lut[LUT_N - 1]
    iw = jnp.arange(WINDOW)
    dist_band = iw[:, None] - jnp.arange(2 * WINDOW)[None, :] + WINDOW
    bias_band = _dist_bias(lut, dist_band, (dist_band >= 0) & (dist_band < WINDOW))
    dist_step = jnp.arange(ts)[:, None] - jnp.arange(wb + ts)[None, :] + wb
    bias_step = _dist_bias(lut, dist_step, (dist_step >= 0) & (dist_step < WINDOW))

    swa_p, swa_s = [], []
    for layer in range(depth):
        if layer == n_a:
            h_kv = _modnorm(x, kv_norm_g.reshape(1, d), mods_kv, 0, tp)
            kvr = _mm(h_kv, w_kv_shared.astype(bf16))
            kv_s = kvr[tp:].reshape(ts, db, 6 * nk).transpose(1, 0, 2)
            nsa_rows_p = kvr[:tp, :4 * nk].reshape(1, tp, 4, KV, HEAD_DIM)
            nsa_rows_s = kv_s[:, :, :4 * nk].reshape(db, ts, 4, KV, HEAD_DIM)
            nsa_win_p = kvr[tp - WINDOW:tp, 4 * nk:].reshape(1, WINDOW, 2, KV, HEAD_DIM)
            nsa_win_s = jnp.concatenate(
                [state_nsa_win, kv_s[:, :, 4 * nk:].reshape(db, ts, 2, KV, HEAD_DIM)], axis=1)[:, -WINDOW:]

            pe_rows = jnp.concatenate([jnp.tile(cmp_pe[0], (1, KV)), jnp.tile(cmp_pe[1], (1, KV))],
                                      axis=1).reshape(BLK, 1, 2 * nk)
            w1b, w2b = cmp_w1.astype(bf16), cmp_w2.astype(bf16)
            npg_p = tp // psz
            cmp_p = _compress(kvr[:tp].reshape(npg_p, psz, 6 * nk), 0,
                              jnp.arange(npg_p, dtype=i32)[None], pe_rows, w1b, w2b)[0]
            pool = cache_nsa.reshape(cache_nsa.shape[0], psz, 4 * nk)
            cmp_s = _compress(pool, 0, page_table, pe_rows, w1b, w2b)
            tail = jnp.zeros((db, psz, 2 * nk), f32).at[:, :ts].set(kv_s[:, :, :2 * nk])
            cmp_t = _compress(tail, 0, jnp.arange(db, dtype=i32)[None], pe_rows, w1b, w2b)[0]
            cmp_t = cmp_t.reshape(2, KV, db, 2, HEAD_DIM)[:, :, :, 0].transpose(2, 0, 1, 3)
            nb_s = past // BLK + 1
            nbp = ((nb_s + 255) // 256) * 256
            cmp_s = jnp.concatenate(
                [cmp_s, cmp_t[:, :, :, None], jnp.zeros((db, 2, KV, nbp - nb_s, HEAD_DIM), f32)], axis=3)
            kc_p, vc_p = cmp_p[0].astype(bf16), cmp_p[1].astype(bf16)
            kc_s, vc_s = cmp_s[:, 0].astype(bf16), cmp_s[:, 1].astype(bf16)

        g_mix = norm_g[layer, 0].reshape(1, d)
        h = _modnorm(x, g_mix, mods, 2 * layer, tp)
        if layer < n_a:
            y = _mm(h, w_qkv_a[layer].astype(bf16))
            q, k, v = y[:, :nq], y[:, nq:nq + nk], y[:, nq + nk:]
            sink = sink_a[layer].astype(f32)
            o_p = _band_attn(_heads_prompt(q, tp).astype(bf16), _kv_prompt(k, tp).astype(bf16),
                             _kv_prompt(v, tp).astype(bf16), bias_band, _head_rows(sink, WINDOW))
            k_s, v_s = _kv_sample(k, tp, ts, db), _kv_sample(v, tp, ts, db)
            kk = jnp.concatenate([state_swa[layer, :, :, 0], k_s], axis=1).transpose(0, 2, 1, 3)
            vv = jnp.concatenate([state_swa[layer, :, :, 1], v_s], axis=1).transpose(0, 2, 1, 3)
            o_s = _step_attn(_heads_sample(q, tp, ts, db).astype(bf16), kk.astype(bf16), vv.astype(bf16),
                             bias_step, _head_rows(sink, ts))
            swa_p.append(jnp.stack([k[tp - wb:tp].reshape(1, wb, KV, HEAD_DIM),
                                    v[tp - wb:tp].reshape(1, wb, KV, HEAD_DIM)], axis=2))
            swa_s.append(jnp.concatenate([state_swa[layer], jnp.stack([k_s, v_s], axis=2)], axis=1)[:, -wb:])
            x = _mm_res(_unheads(o_p, o_s, tp, ts, db), w_o_a[layer].astype(bf16), x, mods, 2 * layer, tp)
        else:
            j = layer - n_a
            qf = _mm(h, w_q_b[j].astype(bf16))
            ngate = 3 * N_HEADS
            wg = jnp.zeros((d, LANES), f32).at[:, :ngate].set(w_gate_b[j]).astype(bf16)
            glog = _mm(h, wg)
            q_p = _heads_prompt(qf, tp).astype(bf16)
            q_s = _heads_sample(qf, tp, ts, db).astype(bf16)

            tq = WINDOW
            nband = tq // BLK + 2
            jj = jnp.arange(nband)
            dist_c = iw[:, None] - (BLK * (jj[None, :] - 2) + BLK - 1)
            bc = _dist_bias(lut, dist_c, dist_c >= 0)
            bc = bc.transpose(0, 2, 1)[..., None]
            oc_p, sel_p = _cmp_prompt(q_p, kc_p, vc_p, bc, _head_rows(c31, tq))
            dist_cs = (past + jnp.arange(ts))[:, None] - (BLK * jnp.arange(nbp)[None, :] + BLK - 1)
            bias_cs = _dist_bias(lut, dist_cs, dist_cs >= 0)
            cur_s = ((past + jnp.arange(ts)) // BLK).astype(i32).reshape(ts, 1)
            oc_s, sel_s = _cmp_sample(q_s, kc_s, vc_s, bias_cs, cur_s)

            tqs = 256
            isl = jnp.arange(tqs)
            dist_sl = isl[:, None] - jnp.arange(2 * tqs)[None, :] + tqs
            bias_sl = _dist_bias(lut, dist_sl, dist_sl >= 0)
            ks_p = _kv_prompt(kvr[:, 2 * nk:3 * nk], tp).astype(bf16)
            vs_p = _kv_prompt(kvr[:, 3 * nk:4 * nk], tp).astype(bf16)
            os_p = _slc_prompt(q_p, ks_p, vs_p, sel_p, bias_sl, _head_rows(c31, tqs), tqs)

            rows = KV * G * ts
            eye = jnp.eye(KV, dtype=bf16)
            qbd = (q_s[:, :, :, None, :] * eye[None, :, None, :, None]).reshape(db, rows, nk)
            selr = jnp.broadcast_to(sel_s[:, :, None], (db, KV, G, ts, nbp)).reshape(db, rows, nbp)
            tk = 2 * psz
            dist_l = (past + jnp.arange(ts))[:, None] - (past - tk + jnp.arange(tk))[None, :]
            bias_l = _dist_bias(lut, dist_l, dist_l >= 0).reshape(rows, tk)
            dist_n = jnp.arange(ts)[:, None] - jnp.arange(BLK)[None, :]
            ok_n = (dist_n >= 0) & (jnp.arange(BLK)[None, :] < ts)
            bias_n = _dist_bias(lut, dist_n, ok_n).reshape(rows, BLK)
            knew = jnp.zeros((db, BLK, nk), f32).at[:, :ts].set(kv_s[:, :, 2 * nk:3 * nk]).astype(bf16)
            vnew = jnp.zeros((db, BLK, nk), f32).at[:, :ts].set(kv_s[:, :, 3 * nk:4 * nk]).astype(bf16)
            os_s = _slc_sample(pool, 2 * nk, page_table, qbd, selr, knew, vnew, bias_l, bias_n,
                               _head_rows(c31, ts).reshape(rows, 1))
            os_s = os_s.reshape(db, KV, G * ts, HEAD_DIM)

            kw_p = _kv_prompt(kvr[:, 4 * nk:5 * nk], tp).astype(bf16)
            vw_p = _kv_prompt(kvr[:, 5 * nk:6 * nk], tp).astype(bf16)
            ow_p = _band_attn(q_p, kw_p, vw_p, bias_band)
            kw_s = kv_s[:, :, 4 * nk:5 * nk].reshape(db, ts, KV, HEAD_DIM)
            vw_s = kv_s[:, :, 5 * nk:6 * nk].reshape(db, ts, KV, HEAD_DIM)
            kk = jnp.concatenate([state_nsa_win[:, :, 0], kw_s], axis=1).transpose(0, 2, 1, 3)
            vv = jnp.concatenate([state_nsa_win[:, :, 1], vw_s], axis=1).transpose(0, 2, 1, 3)
            ow_s = _step_attn(q_s, kk.astype(bf16), vv.astype(bf16), bias_step)

            head_of_lane = jnp.arange(nq) // HEAD_DIM
            expand = (jnp.arange(LANES)[None, :, None]
                      == (jnp.arange(3)[:, None, None] * N_HEADS + head_of_lane[None, None, :])).astype(bf16)
            merged = _merge(glog, _unheads(oc_p, oc_s, tp, ts, db), _unheads(os_p, os_s, tp, ts, db),
                            _unheads(ow_p, ow_s, tp, ts, db), expand)
            x = _mm_res(merged, w_o_b[j].astype(bf16), x, mods, 2 * layer, tp)

        x = _moe_layer(x, norm_g[layer, 1].reshape(1, d), mods, 2 * layer + 1, tp,
                       router_w[layer], router_b[layer], w_up[layer], b_up[layer], w_down[layer], b_down[layer])

    y = _rmsnorm(x, final_norm_g.reshape(1, d))
    y_prompt = y[:tp].reshape(1, tp, d)
    y_sample = y[tp:].reshape(ts, db, d).transpose(1, 0, 2)
    return (y_prompt, y_sample, jnp.stack(swa_p), jnp.stack(swa_s), nsa_rows_p, nsa_rows_s, nsa_win_p, nsa_win_s)
```

```python
import functools
import math

import jax
import jax.numpy as jnp
from jax import lax
from jax.experimental import pallas as pl
from jax.experimental.pallas import tpu as pltpu

f32 = jnp.float32
bf16 = jnp.bfloat16
i32 = jnp.int32

N_HEADS = 32
HEAD_DIM = 64
KV = 4
G = N_HEADS // KV
WINDOW = 128
BLK = 64
N_SEL = 16
TOP_K = 4
SWIGLU_LIMIT = 7.0
SWIGLU_ALPHA = 1.702
REL_BUCKETS = 32
REL_MAX_DIST = 128
EPS = 1e-6
NEG = -1e30
TINY = 1e-20
FORCE_SCORE = 1e4
LUT_N = 256
LANES = 128
VMEM_LIMIT = 56 * 1024 * 1024

TOKEN_TILE = 256
SLC_TQ = 256
SLC_ROWS = 128
SLC_PAGES_PER_TILE = 8
MOE_TM = 1024
MOE_SUB = 256
MOE_FC = 256
MOE_TT = 64


def _cparams(sem, vmem=None):
    return pltpu.CompilerParams(dimension_semantics=sem, vmem_limit_bytes=vmem)


def _col_tile(n, cap=512):
    tn = cap
    while n % tn:
        tn //= 2
    assert tn % LANES == 0
    return tn


def _dot(a, b):
    return jnp.dot(a, b, preferred_element_type=f32)


def _dot_nt(a, b):
    return lax.dot_general(a, b, (((1,), (1,)), ((), ())), preferred_element_type=f32)


def _rel_bucket(dist):
    n = jnp.maximum(dist, 0)
    max_exact = REL_BUCKETS // 2
    nf = jnp.maximum(n, 1).astype(f32)
    large = max_exact + (jnp.log(nf / max_exact) / math.log(REL_MAX_DIST / max_exact)
                         * (REL_BUCKETS - max_exact)).astype(i32)
    return jnp.where(n < max_exact, n, jnp.minimum(large, REL_BUCKETS - 1))


def _dist_bias(lut, dist, ok):
    b = lut[jnp.clip(dist, 0, LUT_N - 1)]
    b = jnp.where(ok[..., None], b, NEG)
    r, c = dist.shape
    return b.reshape(r, c, KV, G).transpose(2, 3, 0, 1).reshape(KV, G * r, c)


def _toeplitz_bias(lut, nrows, ncols, c, window=None, sub=None):
    lw = nrows + ncols
    dist = c + nrows - 1 - jnp.arange(lw)
    ok = dist >= 0
    if window is not None:
        ok = ok & (dist < window)
    vals = lut[jnp.clip(dist, 0, LUT_N - 1)]
    if sub is not None:
        vals = vals - sub[None, :]
    w = jnp.where(ok[:, None], vals, NEG).T
    skew = jnp.tile(w, (1, nrows))[:, :nrows * (lw - 1)].reshape(-1, nrows, lw - 1)
    return skew[:, :, nrows - 1:nrows - 1 + ncols].reshape(KV, G * nrows, ncols)


def _head_rows(vec, r):
    return jnp.broadcast_to(vec.reshape(KV, G, 1, 1), (KV, G, r, 1)).reshape(KV, G * r, 1).astype(f32)


def _ada_kernel(c_ref, w_ref, b_ref, o_ref):
    a = jax.nn.silu(c_ref[...]).astype(bf16)
    o_ref[...] = _dot(a, w_ref[...].astype(bf16)) + b_ref[...]


def _ada(cmat, w, b):
    nm, d, n = w.shape
    r = cmat.shape[0]
    tn = _col_tile(n)
    return pl.pallas_call(
        _ada_kernel,
        out_shape=jax.ShapeDtypeStruct((nm, r, n), f32),
        grid=(nm, n // tn),
        in_specs=[pl.BlockSpec((r, d), lambda m, j: (0, 0)),
                  pl.BlockSpec((None, d, tn), lambda m, j: (m, 0, j)),
                  pl.BlockSpec((None, 1, tn), lambda m, j: (m, 0, j))],
        out_specs=pl.BlockSpec((None, r, tn), lambda m, j: (m, 0, j)),
        compiler_params=_cparams(("arbitrary", "arbitrary")),
        name="ada_mods",
    )(cmat, w, b)


def _modulated(x_ref, g_ref, sh_ref, sc_ref):
    x = x_ref[...]
    tm, d = x.shape
    rg = sh_ref.shape[0]
    y = x * lax.rsqrt(jnp.mean(x * x, axis=-1, keepdims=True) + EPS) * g_ref[...]
    y = y.reshape(tm // rg, rg, d) * (1.0 + sc_ref[...])[None] + sh_ref[...][None]
    return y.reshape(tm, d)


def _modnorm_kernel(x_ref, g_ref, sh_ref, sc_ref, h_ref):
    h_ref[...] = _modulated(x_ref, g_ref, sh_ref, sc_ref).astype(bf16)


def _router_kernel(x_ref, g_ref, sh_ref, sc_ref, whi_ref, wlo_ref, rb_ref, h_ref, ti_ref, tw_ref):
    h = _modulated(x_ref, g_ref, sh_ref, sc_ref)
    hi = h.astype(bf16)
    lo = (h - hi.astype(f32)).astype(bf16)
    h_ref[...] = h
    logits = _dot(hi, whi_ref[...]) + (_dot(lo, whi_ref[...]) + _dot(hi, wlo_ref[...])) + rb_ref[...]
    lane = lax.broadcasted_iota(i32, logits.shape, 1).astype(f32)
    ti = jnp.zeros(logits.shape, f32)
    tv = jnp.zeros(logits.shape, f32)
    v0 = None
    den = None
    for k in range(TOP_K):
        m = jnp.max(logits, axis=-1, keepdims=True)
        idx = jnp.min(jnp.where(logits == m, lane, float(LANES)), axis=-1, keepdims=True)
        if k == 0:
            v0 = m
            e = jnp.ones_like(m)
            den = e
        else:
            e = jnp.exp(m - v0)
            den = den + e
        ti = jnp.where(lane == k, idx, ti)
        tv = jnp.where(lane == k, e, tv)
        logits = jnp.where(lane == idx, -jnp.inf, logits)
    ti_ref[...] = ti.astype(i32)
    tw_ref[...] = tv / den


def _mod_specs(d, rg, n_prompt_tiles, m_idx, cols):
    def spec(col):
        return pl.BlockSpec((None, rg, d), lambda i: (m_idx, jnp.where(i < n_prompt_tiles, 0, 1), col))
    return [spec(c) for c in cols]


def _modnorm(x, g, mods, m_idx, tp, router=None, tm=TOKEN_TILE):
    t, d = x.shape
    rg = mods.shape[1] // 2
    grid = (t // tm,)
    in_specs = [pl.BlockSpec((tm, d), lambda i: (i, 0)),
                pl.BlockSpec((1, d), lambda i: (0, 0))] + _mod_specs(d, rg, tp // tm, m_idx, (0, 1))
    h_spec = pl.BlockSpec((tm, d), lambda i: (i, 0))
    if router is None:
        return pl.pallas_call(
            _modnorm_kernel, out_shape=jax.ShapeDtypeStruct((t, d), bf16), grid=grid, in_specs=in_specs,
            out_specs=h_spec, compiler_params=_cparams(("arbitrary",)), name="modnorm",
        )(x, g, mods, mods)
    whi, wlo, rb = router
    in_specs += [pl.BlockSpec((d, LANES), lambda i: (0, 0)),
                 pl.BlockSpec((d, LANES), lambda i: (0, 0)),
                 pl.BlockSpec((1, LANES), lambda i: (0, 0))]
    lane_spec = pl.BlockSpec((tm, LANES), lambda i: (i, 0))
    return pl.pallas_call(
        _router_kernel,
        out_shape=(jax.ShapeDtypeStruct((t, d), f32), jax.ShapeDtypeStruct((t, LANES), i32),
                   jax.ShapeDtypeStruct((t, LANES), f32)),
        grid=grid, in_specs=in_specs, out_specs=(h_spec, lane_spec, lane_spec),
        compiler_params=_cparams(("arbitrary",)), name="modnorm_router",
    )(x, g, mods, mods, whi, wlo, rb)


def _mm_kernel(a_ref, w_ref, o_ref):
    o_ref[...] = _dot(a_ref[...].astype(bf16), w_ref[...])


def _mm_res_kernel(a_ref, w_ref, x_ref, gt_ref, o_ref):
    acc = _dot(a_ref[...].astype(bf16), w_ref[...])
    tm, tn = acc.shape
    rg = gt_ref.shape[0]
    upd = acc.reshape(tm // rg, rg, tn) * gt_ref[...][None]
    o_ref[...] = x_ref[...] + upd.reshape(tm, tn)


def _mm(a, w, tm=TOKEN_TILE, tn=512):
    t, k = a.shape
    n = w.shape[1]
    tn = _col_tile(n, tn)
    return pl.pallas_call(
        _mm_kernel,
        out_shape=jax.ShapeDtypeStruct((t, n), f32),
        grid=(n // tn, t // tm),
        in_specs=[pl.BlockSpec((tm, k), lambda j, i: (i, 0)),
                  pl.BlockSpec((k, tn), lambda j, i: (0, j))],
        out_specs=pl.BlockSpec((tm, tn), lambda j, i: (i, j)),
        compiler_params=_cparams(("arbitrary", "arbitrary")),
        name="matmul",
    )(a, w)


def _mm_res(a, w, x, mods, m_idx, tp, tm=TOKEN_TILE, tn=512):
    t, k = a.shape
    n = w.shape[1]
    tn = _col_tile(n, tn)
    rg = mods.shape[1] // 2
    npt = tp // tm
    nj = n // tn
    return pl.pallas_call(
        _mm_res_kernel,
        out_shape=jax.ShapeDtypeStruct((t, n), f32),
        grid=(nj, t // tm),
        in_specs=[pl.BlockSpec((tm, k), lambda j, i: (i, 0)),
                  pl.BlockSpec((k, tn), lambda j, i: (0, j)),
                  pl.BlockSpec((tm, tn), lambda j, i: (i, j)),
                  pl.BlockSpec((None, rg, tn), lambda j, i: (m_idx, jnp.where(i < npt, 0, 1), 2 * nj + j))],
        out_specs=pl.BlockSpec((tm, tn), lambda j, i: (i, j)),
        compiler_params=_cparams(("arbitrary", "arbitrary")),
        name="matmul_residual",
    )(a, w, x, mods)


def _band_kernel(q_ref, kp_ref, kc_ref, vp_ref, vc_ref, b_ref, *rest, has_sink):
    if has_sink:
        s_ref, o_ref = rest
    else:
        (o_ref,) = rest
    g, tq, hd = q_ref.shape
    q = q_ref[...].reshape(g * tq, hd)
    kk = jnp.concatenate([kp_ref[...], kc_ref[...]], axis=0)
    vv = jnp.concatenate([vp_ref[...], vc_ref[...]], axis=0)
    s = _dot_nt(q, kk) * (hd ** -0.5) + b_ref[...]
    col = lax.broadcasted_iota(i32, s.shape, 1)
    first = pl.program_id(1) == 0
    s = jnp.where(jnp.logical_and(first, col < tq), NEG, s)
    m = jnp.max(s, axis=-1, keepdims=True)
    if has_sink:
        m = jnp.maximum(m, s_ref[...])
    e = jnp.exp(s - m)
    den = jnp.sum(e, axis=-1, keepdims=True)
    if has_sink:
        den = den + jnp.exp(s_ref[...] - m)
    o = _dot(e.astype(bf16), vv) / den
    o_ref[...] = o.reshape(g, tq, hd)


def _band_attn(q, k, v, bias, sink=None):
    _, g, t, hd = q.shape
    w = WINDOW
    kspec_p = pl.BlockSpec((None, w, hd), lambda h, i: (h, jnp.maximum(i - 1, 0), 0))
    kspec_c = pl.BlockSpec((None, w, hd), lambda h, i: (h, i, 0))
    in_specs = [pl.BlockSpec((None, g, w, hd), lambda h, i: (h, 0, i, 0)),
                kspec_p, kspec_c, kspec_p, kspec_c,
                pl.BlockSpec((None, g * w, 2 * w), lambda h, i: (h, 0, 0))]
    args = [q, k, k, v, v, bias]
    if sink is not None:
        in_specs.append(pl.BlockSpec((None, g * w, 1), lambda h, i: (h, 0, 0)))
        args.append(sink)
    return pl.pallas_call(
        functools.partial(_band_kernel, has_sink=sink is not None),
        out_shape=jax.ShapeDtypeStruct((KV, g, t, hd), f32),
        grid=(KV, t // w),
        in_specs=in_specs,
        out_specs=pl.BlockSpec((None, g, w, hd), lambda h, i: (h, 0, i, 0)),
        compiler_params=_cparams(("arbitrary", "arbitrary")),
        name="band_attn",
    )(*args)


def _step_kernel(q_ref, k_ref, v_ref, b_ref, *rest, has_sink):
    if has_sink:
        s_ref, o_ref = rest
    else:
        (o_ref,) = rest
    hd = q_ref.shape[-1]
    for h in range(KV):
        s = _dot_nt(q_ref[h], k_ref[h]) * (hd ** -0.5) + b_ref[h]
        m = jnp.max(s, axis=-1, keepdims=True)
        if has_sink:
            m = jnp.maximum(m, s_ref[h])
        e = jnp.exp(s - m)
        den = jnp.sum(e, axis=-1, keepdims=True)
        if has_sink:
            den = den + jnp.exp(s_ref[h] - m)
        o_ref[h] = _dot(e.astype(bf16), v_ref[h]) / den


def _step_attn(q, k, v, bias, sink=None):
    db, _, r, hd = q.shape
    l = k.shape[2]
    in_specs = [pl.BlockSpec((None, KV, r, hd), lambda b: (b, 0, 0, 0)),
                pl.BlockSpec((None, KV, l, hd), lambda b: (b, 0, 0, 0)),
                pl.BlockSpec((None, KV, l, hd), lambda b: (b, 0, 0, 0)),
                pl.BlockSpec((KV, r, l), lambda b: (0, 0, 0))]
    args = [q, k, v, bias]
    if sink is not None:
        in_specs.append(pl.BlockSpec((KV, r, 1), lambda b: (0, 0, 0)))
        args.append(sink)
    return pl.pallas_call(
        functools.partial(_step_kernel, has_sink=sink is not None),
        out_shape=jax.ShapeDtypeStruct((db, KV, r, hd), f32),
        grid=(db,),
        in_specs=in_specs,
        out_specs=pl.BlockSpec((None, KV, r, hd), lambda b: (b, 0, 0, 0)),
        compiler_params=_cparams(("arbitrary",)),
        name="step_attn",
    )(*args)


def _compress_kernel(pt_ref, pool_ref, pe_ref, w1_hbm, w2_ref, o_ref, buf, sem, w1_s, wsem, acc,
                     *, ch0, pgs, nchunk):
    b = pl.program_id(0)
    c = pl.program_id(1)
    step = b * nchunk + c
    nsteps = pl.num_programs(0) * nchunk
    slot = step % 2
    hd = buf.shape[3]
    nhid = w2_ref.shape[1]

    def page_copy(bb, cc, p, sl):
        page = pt_ref[bb, cc * pgs + p]
        return pltpu.make_async_copy(pool_ref.at[page, pl.ds(ch0, 2)], buf.at[sl, :, :, :, p, :], sem.at[sl])

    def start(st, sl):
        bb = st // nchunk
        cc = st % nchunk

        def body(p, carry):
            page_copy(bb, cc, p, sl).start()
            return carry
        lax.fori_loop(0, pgs, body, 0)

    @pl.when(step == 0)
    def _():
        w1_copy = pltpu.make_async_copy(w1_hbm, w1_s, wsem.at[0])
        w1_copy.start()
        start(step, slot)
        w1_copy.wait()

    @pl.when(step + 1 < nsteps)
    def _():
        start(step + 1, 1 - slot)

    def wait_body(p, carry):
        page_copy(b, c, p, slot).wait()
        return carry
    lax.fori_loop(0, pgs, wait_body, 0)

    for ch in range(2):
        for dp in range(hd // 2):
            parts = []
            for h in range(KV):
                pair = [(buf[slot, ch, h, 2 * dp + dd] + pe_ref[ch, 2 * dp + dd]).astype(bf16) for dd in range(2)]
                parts.append(jnp.concatenate(pair, axis=1))
            part = _dot(jnp.concatenate(parts, axis=0), w1_s[ch, dp])
            if dp == 0:
                acc[...] = part
            else:
                acc[...] += part
        hid = jax.nn.gelu(acc[...])
        for n in range(2):
            out = _dot(hid[:, n * nhid:(n + 1) * nhid].astype(bf16), w2_ref[ch])
            o_ref[ch, n] = out.reshape(KV, pgs, hd)


def _compress(pool, ch0, pt, pe_t, w1p, w2):
    nseq, npg = pt.shape
    psz = pool.shape[-1]
    pgs = min(npg, 64)
    nchunk = npg // pgs
    nhid = w2.shape[1]
    kern = functools.partial(_compress_kernel, ch0=ch0, pgs=pgs, nchunk=nchunk)
    out = pl.pallas_call(
        kern,
        out_shape=jax.ShapeDtypeStruct((nseq, 2, 2, KV, npg, HEAD_DIM), f32),
        grid_spec=pltpu.PrefetchScalarGridSpec(
            num_scalar_prefetch=1,
            grid=(nseq, nchunk),
            in_specs=[pl.BlockSpec(memory_space=pl.ANY),
                      pl.BlockSpec(pe_t.shape, lambda b, c, pt: (0, 0, 0, 0)),
                      pl.BlockSpec(memory_space=pl.ANY),
                      pl.BlockSpec(w2.shape, lambda b, c, pt: (0, 0, 0))],
            out_specs=pl.BlockSpec((None, 2, 2, KV, pgs, HEAD_DIM), lambda b, c, pt: (b, 0, 0, 0, c, 0)),
            scratch_shapes=[pltpu.VMEM((2, 2, KV, HEAD_DIM, pgs, psz), f32),
                            pltpu.SemaphoreType.DMA((2,)),
                            pltpu.VMEM(w1p.shape, bf16),
                            pltpu.SemaphoreType.DMA((1,)),
                            pltpu.VMEM((KV * pgs, 2 * nhid), f32)]),
        compiler_params=_cparams(("arbitrary", "arbitrary"), VMEM_LIMIT),
        name="compress",
    )(pt, pool, pe_t, w1p, w2)
    return out.transpose(0, 1, 3, 4, 2, 5).reshape(nseq, 2, KV, 2 * npg, HEAD_DIM)


def _cmp_core(q, kc, vc, bias, cur, g, tq):
    hd = q.shape[-1]
    nb = kc.shape[0]
    vis = bias > 0.5 * NEG
    s = _dot_nt(q, kc) * (hd ** -0.5) + bias
    m = jnp.max(s, axis=-1, keepdims=True)
    e = jnp.where(vis, jnp.exp(s - m), 0.0)
    p = e / jnp.maximum(jnp.sum(e, axis=-1, keepdims=True), TINY)
    o = _dot(p.astype(bf16), vc)
    imp = jnp.sum(p.reshape(g, tq, nb), axis=0)
    col = lax.broadcasted_iota(i32, (tq, nb), 1)
    colf = col.astype(f32)
    forced = (col == 0) | (col == cur) | (col == cur - 1)
    imp = jnp.where(forced, FORCE_SCORE, imp)
    imp = jnp.where(col <= cur, imp, NEG)
    sel = jnp.zeros((tq, nb), f32)
    for _ in range(N_SEL):
        mx = jnp.max(imp, axis=-1, keepdims=True)
        idx = jnp.min(jnp.where(imp == mx, colf, float(nb)), axis=-1, keepdims=True)
        hit = colf == idx
        sel = jnp.where(hit, 1.0, sel)
        imp = jnp.where(hit, -jnp.inf, imp)
    return o, sel


def _cmp_prompt_kernel(q_ref, kc_ref, vc_ref, bc_ref, c31_ref, o_ref, sel_ref):
    g, tq, hd = q_ref.shape
    nb = kc_ref.shape[0]
    qi = pl.program_id(1)
    rows = g * tq
    col = lax.broadcasted_iota(i32, (rows, nb), 1)
    first = (tq // BLK) * qi - 2
    nband = bc_ref.shape[0]
    bias = jnp.where(col < first, c31_ref[...], NEG)
    for jj in range(nband):
        bias = jnp.where(col == first + jj, bc_ref[jj], bias)
    t = qi * tq + lax.broadcasted_iota(i32, (tq, 1), 0)
    o, sel = _cmp_core(q_ref[...].reshape(rows, hd), kc_ref[...], vc_ref[...], bias, t // BLK, g, tq)
    o_ref[...] = o.reshape(g, tq, hd)
    sel_ref[...] = sel.astype(bf16)


def _cmp_prompt(q, kc, vc, bc, c31):
    _, g, t, hd = q.shape
    nb = kc.shape[1]
    tq = WINDOW
    nband = bc.shape[1]
    return pl.pallas_call(
        _cmp_prompt_kernel,
        out_shape=(jax.ShapeDtypeStruct((KV, g, t, hd), f32), jax.ShapeDtypeStruct((KV, t, nb), bf16)),
        grid=(KV, t // tq),
        in_specs=[pl.BlockSpec((None, g, tq, hd), lambda h, i: (h, 0, i, 0)),
                  pl.BlockSpec((None, nb, hd), lambda h, i: (h, 0, 0)),
                  pl.BlockSpec((None, nb, hd), lambda h, i: (h, 0, 0)),
                  pl.BlockSpec((None, nband, g * tq, 1), lambda h, i: (h, 0, 0, 0)),
                  pl.BlockSpec((None, g * tq, 1), lambda h, i: (h, 0, 0))],
        out_specs=(pl.BlockSpec((None, g, tq, hd), lambda h, i: (h, 0, i, 0)),
                   pl.BlockSpec((None, tq, nb), lambda h, i: (h, i, 0))),
        compiler_params=_cparams(("arbitrary", "arbitrary")),
        name="cmp_prompt",
    )(q, kc, vc, bc, c31)


def _cmp_sample_kernel(q_ref, kc_ref, vc_ref, b_ref, cur_ref, o_ref, sel_ref):
    ts = cur_ref.shape[0]
    g = q_ref.shape[1] // ts
    for h in range(KV):
        o, sel = _cmp_core(q_ref[h], kc_ref[h], vc_ref[h], b_ref[h], cur_ref[...], g, ts)
        o_ref[h] = o
        sel_ref[h] = sel.astype(bf16)


def _cmp_sample(q, kc, vc, bias, cur):
    db, _, r, hd = q.shape
    nbp = kc.shape[2]
    ts = cur.shape[0]
    return pl.pallas_call(
        _cmp_sample_kernel,
        out_shape=(jax.ShapeDtypeStruct((db, KV, r, hd), f32), jax.ShapeDtypeStruct((db, KV, ts, nbp), bf16)),
        grid=(db,),
        in_specs=[pl.BlockSpec((None, KV, r, hd), lambda b: (b, 0, 0, 0)),
                  pl.BlockSpec((None, KV, nbp, hd), lambda b: (b, 0, 0, 0)),
                  pl.BlockSpec((None, KV, nbp, hd), lambda b: (b, 0, 0, 0)),
                  pl.BlockSpec((KV, r, nbp), lambda b: (0, 0, 0)),
                  pl.BlockSpec((ts, 1), lambda b: (0, 0))],
        out_specs=(pl.BlockSpec((None, KV, r, hd), lambda b: (b, 0, 0, 0)),
                   pl.BlockSpec((None, KV, ts, nbp), lambda b: (b, 0, 0, 0))),
        compiler_params=_cparams(("arbitrary",)),
        name="cmp_sample",
    )(q, kc, vc, bias, cur)


def _softmax_chunk(a, vt_dot, m_ref, l_ref, acc_ref, rows):
    m_old = m_ref[rows, :]
    m_new = jnp.maximum(m_old, jnp.max(a, axis=-1, keepdims=True))
    alpha = jnp.exp(m_old - m_new)
    e = jnp.exp(a - m_new)
    l_ref[rows, :] = alpha * l_ref[rows, :] + jnp.sum(e, axis=-1, keepdims=True)
    acc_ref[rows, :] = alpha * acc_ref[rows, :] + vt_dot(e.astype(bf16))
    m_ref[rows, :] = m_new


def _slc_prompt_kernel(q_ref, kt_ref, v_ref, sel_ref, ex_ref, b_ref, o_ref,
                       s_scr, p_scr, madd_scr, m_scr, mt_scr, al_scr, acc_scr):
    g, tq, hd = q_ref.shape
    qi = pl.program_id(1)
    rows = g * tq
    rc = SLC_ROWS
    nchunk = rows // rc
    per_head = tq // rc

    def tile(kstart, width, bias_col):
        kt = kt_ref[:, pl.ds(kstart, width)]
        vt = v_ref[pl.ds(kstart, width), :]
        s_scr[:, :width] = _dot(q_ref[...].reshape(rows, hd), kt)
        msk = _dot(sel_ref[...], ex_ref[:, pl.ds(kstart, width)])
        madd_scr[:, :width] = jnp.where(msk > 0.5, 0.0, NEG)

        def logits(ci):
            part = ci % per_head
            a = s_scr[ci * rc:(ci + 1) * rc, :width] + madd_scr[part * rc:(part + 1) * rc, :width]
            if bias_col is not None:
                a = a + b_ref[ci * rc:(ci + 1) * rc, bias_col:bias_col + width]
            return a

        for ci in range(nchunk):
            mt_scr[ci * rc:(ci + 1) * rc, :] = jnp.broadcast_to(
                jnp.max(logits(ci), axis=-1, keepdims=True), (rc, LANES))
        m_old = m_scr[...]
        m_new = jnp.maximum(m_old, mt_scr[...])
        al_scr[...] = jnp.exp(m_old - m_new)
        m_scr[...] = m_new
        for ci in range(nchunk):
            m_c = m_scr[ci * rc:(ci + 1) * rc, :]
            e = jnp.exp(logits(ci) - jnp.tile(m_c, (1, width // LANES)))
            p_scr[ci * rc:(ci + 1) * rc, :width] = e.astype(bf16)
        acc_scr[...] = al_scr[...] * acc_scr[...] + _dot(p_scr[:, :width], vt)

    m_scr[...] = jnp.full(m_scr.shape, NEG, f32)
    acc_scr[...] = jnp.zeros(acc_scr.shape, f32)

    @pl.when(qi == 0)
    def _():
        tile(0, tq, tq)

    @pl.when(qi > 0)
    def _():
        tile(pl.multiple_of((qi - 1) * tq, tq), 2 * tq, 0)

    nfar = jnp.maximum(qi - 1, 0)

    def far(j, carry):
        tile(pl.multiple_of(j * 2 * tq, 2 * tq), 2 * tq, None)
        return carry
    lax.fori_loop(0, nfar // 2, far, 0)

    @pl.when(nfar % 2 == 1)
    def _():
        tile(pl.multiple_of((nfar - 1) * tq, tq), tq, None)

    acc = acc_scr[...]
    o_ref[...] = (acc[:, :hd] / acc[:, hd:hd + 1]).reshape(g, tq, hd)


def _slc_prompt(q, kt, v1, sel, expand, bias, tq):
    _, g, t, hd = q.shape
    nb = sel.shape[-1]
    rows = g * tq
    return pl.pallas_call(
        _slc_prompt_kernel,
        out_shape=jax.ShapeDtypeStruct((KV, g, t, hd), f32),
        grid=(KV, t // tq),
        in_specs=[pl.BlockSpec((None, g, tq, hd), lambda h, i: (h, 0, i, 0)),
                  pl.BlockSpec((None, hd, t), lambda h, i: (h, 0, 0)),
                  pl.BlockSpec((None, t, LANES), lambda h, i: (h, 0, 0)),
                  pl.BlockSpec((None, tq, nb), lambda h, i: (h, i, 0)),
                  pl.BlockSpec((nb, t), lambda h, i: (0, 0)),
                  pl.BlockSpec((None, rows, 2 * tq), lambda h, i: (h, 0, 0))],
        out_specs=pl.BlockSpec((None, g, tq, hd), lambda h, i: (h, 0, i, 0)),
        scratch_shapes=[pltpu.VMEM((rows, 2 * tq), f32), pltpu.VMEM((rows, 2 * tq), bf16),
                        pltpu.VMEM((tq, 2 * tq), f32), pltpu.VMEM((rows, LANES), f32),
                        pltpu.VMEM((rows, LANES), f32), pltpu.VMEM((rows, LANES), f32),
                        pltpu.VMEM((rows, LANES), f32)],
        compiler_params=_cparams(("arbitrary", "arbitrary"), VMEM_LIMIT),
        name="slc_prompt",
    )(q, kt, v1, sel, expand, bias)


def _slc_sample_kernel(pt_ref, pool_ref, q_ref, sel_ref, ex_ref, kn_ref, vn_ref, bl_ref, bn_ref, exn_ref, o_ref,
                       buf, sem, madd_ref, m_ref, l_ref, acc_ref, *, ch0, pgs, nchunk, ppt):
    b = pl.program_id(0)
    c = pl.program_id(1)
    step = b * nchunk + c
    nsteps = pl.num_programs(0) * nchunk
    slot = step % 2
    psz = buf.shape[-1]
    rows = q_ref.shape[0]
    ntile = pgs // ppt
    tk = ppt * psz
    last = c == nchunk - 1

    def page_copy(bb, cc, p, sl):
        page = pt_ref[bb, cc * pgs + p]
        return pltpu.make_async_copy(pool_ref.at[page, pl.ds(ch0, 2)], buf.at[sl, p], sem.at[sl])

    def start(st, sl):
        bb = st // nchunk
        cc = st % nchunk

        def body(p, carry):
            page_copy(bb, cc, p, sl).start()
            return carry
        lax.fori_loop(0, pgs, body, 0)

    @pl.when(step == 0)
    def _():
        start(step, slot)

    @pl.when(step + 1 < nsteps)
    def _():
        start(step + 1, 1 - slot)

    def wait_body(p, carry):
        page_copy(b, c, p, slot).wait()
        return carry
    lax.fori_loop(0, pgs, wait_body, 0)

    @pl.when(c == 0)
    def _():
        m_ref[...] = jnp.full(m_ref.shape, NEG, f32)
        l_ref[...] = jnp.zeros(l_ref.shape, f32)
        acc_ref[...] = jnp.zeros(acc_ref.shape, f32)

    madd_ref[...] = jnp.where(_dot(sel_ref[...], ex_ref[...]) > 0.5, 0.0, NEG)

    def tile(j, with_bias):
        kt = jnp.concatenate([buf[slot, j * ppt + p, 0] for p in range(ppt)], axis=1).astype(bf16)
        vt = jnp.concatenate([buf[slot, j * ppt + p, 1] for p in range(ppt)], axis=1).astype(bf16)
        col0 = pl.multiple_of(j * tk, tk)
        a = _dot(q_ref[...], kt) + madd_ref[:, pl.ds(col0, tk)]
        if with_bias:
            a = a + bl_ref[...]
        _softmax_chunk(a, lambda p: _dot_nt(p, vt), m_ref, l_ref, acc_ref, slice(None))

    def plain(j, carry):
        tile(j, False)
        return carry
    lax.fori_loop(0, ntile - jnp.where(last, 1, 0), plain, 0)

    @pl.when(last)
    def _():
        tile(ntile - 1, True)
        madd_new = jnp.where(_dot(sel_ref[...], exn_ref[...]) > 0.5, 0.0, NEG)
        vn = vn_ref[...]
        a = _dot(q_ref[...], kn_ref[...]) + madd_new + bn_ref[...]
        _softmax_chunk(a, lambda p: _dot_nt(p, vn), m_ref, l_ref, acc_ref, slice(None))
        full = acc_ref[...] / l_ref[...]
        r = rows // KV
        o_ref[...] = jnp.concatenate(
            [full[h * r:(h + 1) * r, h * HEAD_DIM:(h + 1) * HEAD_DIM] for h in range(KV)], axis=0)


def _slc_sample(pool, ch0, pt, qbd, selr, expand, knew, vnew, bias_last, bias_new, expand_new, ppt=4):
    db, npg = pt.shape
    psz = pool.shape[-1]
    rows = qbd.shape[1]
    nkd = KV * HEAD_DIM
    pgs = min(npg, 32)
    nchunk = npg // pgs
    ppt = min(ppt, pgs)
    nbp = selr.shape[-1]
    ck = pgs * psz
    kern = functools.partial(_slc_sample_kernel, ch0=ch0, pgs=pgs, nchunk=nchunk, ppt=ppt)
    return pl.pallas_call(
        kern,
        out_shape=jax.ShapeDtypeStruct((db, rows, HEAD_DIM), f32),
        grid_spec=pltpu.PrefetchScalarGridSpec(
            num_scalar_prefetch=1,
            grid=(db, nchunk),
            in_specs=[pl.BlockSpec(memory_space=pl.ANY),
                      pl.BlockSpec((None, rows, nkd), lambda b, c, pt: (b, 0, 0)),
                      pl.BlockSpec((None, rows, nbp), lambda b, c, pt: (b, 0, 0)),
                      pl.BlockSpec((nbp, ck), lambda b, c, pt: (0, c)),
                      pl.BlockSpec((None, nkd, BLK), lambda b, c, pt: (b, 0, 0)),
                      pl.BlockSpec((None, nkd, BLK), lambda b, c, pt: (b, 0, 0)),
                      pl.BlockSpec((rows, ppt * psz), lambda b, c, pt: (0, 0)),
                      pl.BlockSpec((rows, BLK), lambda b, c, pt: (0, 0)),
                      pl.BlockSpec((nbp, BLK), lambda b, c, pt: (0, 0))],
            out_specs=pl.BlockSpec((None, rows, HEAD_DIM), lambda b, c, pt: (b, 0, 0)),
            scratch_shapes=[pltpu.VMEM((2, pgs, 2, nkd, psz), f32),
                            pltpu.SemaphoreType.DMA((2,)),
                            pltpu.VMEM((rows, ck), f32),
                            pltpu.VMEM((rows, 1), f32), pltpu.VMEM((rows, 1), f32),
                            pltpu.VMEM((rows, nkd), f32)]),
        compiler_params=_cparams(("arbitrary", "arbitrary"), VMEM_LIMIT),
        name="slc_sample",
    )(pt, pool, qbd, selr, expand, knew, vnew, bias_last, bias_new, expand_new)


def _merge_kernel(gl_ref, oc_ref, os_ref, ow_ref, ex_ref, o_ref):
    gate = jax.nn.sigmoid(gl_ref[...])
    hi = gate.astype(bf16)
    lo = (gate - hi.astype(f32)).astype(bf16)
    acc = None
    for br, ref in enumerate((oc_ref, os_ref, ow_ref)):
        ex = ex_ref[br]
        gx = _dot(hi, ex) + _dot(lo, ex)
        term = gx * ref[...]
        acc = term if acc is None else acc + term
    o_ref[...] = acc.astype(bf16)


def _merge(glog, oc, osl, ow, expand, tm=TOKEN_TILE):
    t, n = oc.shape
    return pl.pallas_call(
        _merge_kernel,
        out_shape=jax.ShapeDtypeStruct((t, n), bf16),
        grid=(t // tm,),
        in_specs=[pl.BlockSpec((tm, LANES), lambda i: (i, 0)),
                  pl.BlockSpec((tm, n), lambda i: (i, 0)),
                  pl.BlockSpec((tm, n), lambda i: (i, 0)),
                  pl.BlockSpec((tm, n), lambda i: (i, 0)),
                  pl.BlockSpec((3, LANES, n), lambda i: (0, 0, 0))],
        out_specs=pl.BlockSpec((tm, n), lambda i: (i, 0)),
        compiler_params=_cparams(("arbitrary",)),
        name="nsa_merge",
    )(glog, oc, osl, ow, expand)


def _moe_kernel(te_ref, ns_ref, ic_ref, in_ref, h_ref, wg_ref, wl_ref, bg_ref, bl_ref, wd_ref, bd_ref, o_ref,
                xbuf, sem, xs, wg_s, wl_s, wd_s, *, sub):
    t = pl.program_id(0)
    f = pl.program_id(1)
    nt = pl.num_programs(0)
    nsub = ns_ref[t]

    def row_copy(idx_ref, r):
        return pltpu.make_async_copy(h_ref.at[pl.ds(idx_ref[0, r], 1)], xbuf.at[pl.ds(r, 1)], sem.at[0])

    def start(idx_ref, nrows):
        def body(r, carry):
            row_copy(idx_ref, r).start()
            return carry
        lax.fori_loop(0, nrows, body, 0)

    @pl.when(f == 0)
    def _():
        o_ref[...] = jnp.zeros(o_ref.shape, f32)

        @pl.when(jnp.logical_and(t == 0, nsub > 0))
        def _():
            start(ic_ref, nsub * sub)

        @pl.when(nsub > 0)
        def _():
            def wait_body(r, carry):
                row_copy(ic_ref, r).wait()
                return carry
            lax.fori_loop(0, nsub * sub, wait_body, 0)

            def cast_body(s, carry):
                r = pl.multiple_of(s * sub, sub)
                xs[pl.ds(r, sub), :] = xbuf[pl.ds(r, sub), :].astype(bf16)
                return carry
            lax.fori_loop(0, nsub, cast_body, 0)

        nnext = ns_ref[jnp.minimum(t + 1, nt - 1)]

        @pl.when(jnp.logical_and(t + 1 < nt, nnext > 0))
        def _():
            start(in_ref, nnext * sub)

    @pl.when(nsub > 0)
    def _():
        wg_s[...] = wg_ref[...].astype(bf16)
        wl_s[...] = wl_ref[...].astype(bf16)
        wd_s[...] = wd_ref[...].astype(bf16)
        first = jnp.where(f == 0, 1.0, 0.0)

        def body(s, carry):
            r = pl.multiple_of(s * sub, sub)
            x = xs[pl.ds(r, sub), :]
            glu = jnp.minimum(_dot(x, wg_s[...]) + bg_ref[...], SWIGLU_LIMIT)
            lin = jnp.clip(_dot(x, wl_s[...]) + bl_ref[...], -SWIGLU_LIMIT, SWIGLU_LIMIT)
            act = glu * jax.nn.sigmoid(SWIGLU_ALPHA * glu) * (lin + 1.0)
            y = _dot(act.astype(bf16), wd_s[...])
            o_ref[pl.ds(r, sub), :] += y + first * bd_ref[...]
            return carry
        lax.fori_loop(0, nsub, body, 0)


def _moe_experts(h, src_tok, tile_e, tile_ns, layer, w_up, b_up, w_down, b_down):
    t, d = h.shape
    nt, _, tm = src_tok.shape
    ff = w_down.shape[2]
    fc = min(MOE_FC, ff)
    nf = ff // fc

    def fidx(ti, f, ns):
        return jnp.where(ns[ti] > 0, f, 0)

    idx_c = pl.BlockSpec((None, 1, tm), lambda ti, f, te, ns: (ti, 0, 0), memory_space=pltpu.SMEM)
    idx_n = pl.BlockSpec((None, 1, tm), lambda ti, f, te, ns: (jnp.minimum(ti + 1, nt - 1), 0, 0),
                         memory_space=pltpu.SMEM)
    return pl.pallas_call(
        functools.partial(_moe_kernel, sub=MOE_SUB),
        out_shape=jax.ShapeDtypeStruct((nt * tm, d), f32),
        grid_spec=pltpu.PrefetchScalarGridSpec(
            num_scalar_prefetch=2,
            grid=(nt, nf),
            in_specs=[idx_c, idx_n,
                      pl.BlockSpec(memory_space=pl.ANY),
                      pl.BlockSpec((None, None, d, fc), lambda ti, f, te, ns: (layer, te[ti], 0, fidx(ti, f, ns))),
                      pl.BlockSpec((None, None, d, fc),
                                   lambda ti, f, te, ns: (layer, te[ti], 0, nf + fidx(ti, f, ns))),
                      pl.BlockSpec((None, None, 1, fc), lambda ti, f, te, ns: (layer, te[ti], 0, fidx(ti, f, ns))),
                      pl.BlockSpec((None, None, 1, fc),
                                   lambda ti, f, te, ns: (layer, te[ti], 0, nf + fidx(ti, f, ns))),
                      pl.BlockSpec((None, None, fc, d), lambda ti, f, te, ns: (layer, te[ti], fidx(ti, f, ns), 0)),
                      pl.BlockSpec((None, None, 1, d), lambda ti, f, te, ns: (layer, te[ti], 0, 0))],
            out_specs=pl.BlockSpec((tm, d), lambda ti, f, te, ns: (ti, 0)),
            scratch_shapes=[pltpu.VMEM((tm, d), f32), pltpu.SemaphoreType.DMA((1,)), pltpu.VMEM((tm, d), bf16),
                            pltpu.VMEM((d, fc), bf16), pltpu.VMEM((d, fc), bf16), pltpu.VMEM((fc, d), bf16)]),
        compiler_params=_cparams(("arbitrary", "arbitrary"), VMEM_LIMIT),
        name="moe_experts",
    )(tile_e, tile_ns, src_tok, src_tok, h, w_up, w_up, b_up, b_up, w_down, b_down)


def _combine_kernel(ic_ref, in_ref, y_ref, x_ref, tw_ref, gt_ref, o_ref, buf, sem):
    step = pl.program_id(0)
    nsteps = pl.num_programs(0)
    slot = step % 2
    nrow = buf.shape[1]

    def row_copy(idx_ref, r, sl):
        return pltpu.make_async_copy(y_ref.at[pl.ds(idx_ref[0, r], 1)], buf.at[sl, pl.ds(r, 1)], sem.at[sl])

    def start(idx_ref, sl):
        def body(r, carry):
            row_copy(idx_ref, r, sl).start()
            return carry
        lax.fori_loop(0, nrow, body, 0)

    @pl.when(step == 0)
    def _():
        start(ic_ref, slot)

    @pl.when(step + 1 < nsteps)
    def _():
        start(in_ref, 1 - slot)

    def wait_body(r, carry):
        row_copy(ic_ref, r, slot).wait()
        return carry
    lax.fori_loop(0, nrow, wait_body, 0)

    tt, d = x_ref.shape
    rg = gt_ref.shape[0]
    tw = tw_ref[...]
    acc = None
    for k in range(TOP_K):
        term = tw[:, k:k + 1] * buf[slot, pl.ds(k * tt, tt), :]
        acc = term if acc is None else acc + term
    if tt >= rg:
        upd = (acc.reshape(tt // rg, rg, d) * gt_ref[...][None]).reshape(tt, d)
    else:
        part = step % (rg // tt)
        upd = acc * gt_ref[pl.ds(pl.multiple_of(part * tt, tt), tt), :]
    o_ref[...] = x_ref[...] + upd


def _moe_combine(y, pos, x, topw, mods, m_idx, tp):
    t, d = x.shape
    tt = MOE_TT
    nsteps = t // tt
    rg = mods.shape[1] // 2
    npt = tp // tt
    n_idx = TOP_K * tt
    idx_c = pl.BlockSpec((None, 1, n_idx), lambda i: (i, 0, 0), memory_space=pltpu.SMEM)
    idx_n = pl.BlockSpec((None, 1, n_idx), lambda i: (jnp.minimum(i + 1, nsteps - 1), 0, 0),
                         memory_space=pltpu.SMEM)
    return pl.pallas_call(
        _combine_kernel,
        out_shape=jax.ShapeDtypeStruct((t, d), f32),
        grid=(nsteps,),
        in_specs=[idx_c, idx_n,
                  pl.BlockSpec(memory_space=pl.ANY),
                  pl.BlockSpec((tt, d), lambda i: (i, 0)),
                  pl.BlockSpec((tt, LANES), lambda i: (i, 0)),
                  pl.BlockSpec((None, rg, d), lambda i: (m_idx, jnp.where(i < npt, 0, 1), 2))],
        out_specs=pl.BlockSpec((tt, d), lambda i: (i, 0)),
        scratch_shapes=[pltpu.VMEM((2, n_idx, d), f32), pltpu.SemaphoreType.DMA((2,))],
        compiler_params=_cparams(("arbitrary",)),
        name="moe_combine",
    )(pos, pos, y, x, topw, mods)


def _moe_layer(x, g, mods, m_idx, tp, layer, router_w, router_b, w_up, b_up, w_down, b_down):
    t, d = x.shape
    ne = router_w.shape[1]
    ff = w_down.shape[2]
    tm, sub, tt = MOE_TM, MOE_SUB, MOE_TT

    rw = jnp.pad(router_w, ((0, 0), (0, LANES - ne)))
    whi = rw.astype(bf16)
    wlo = (rw - whi.astype(f32)).astype(bf16)
    rb = jnp.pad(router_b.reshape(1, ne), ((0, 0), (0, LANES - ne)), constant_values=-jnp.inf)
    h, topi, topw = _modnorm(x, g, mods, m_idx, tp, router=(whi, wlo, rb))

    e_flat = topi[:, :TOP_K].reshape(-1)
    npair = t * TOP_K
    onehot = (e_flat[:, None] == jnp.arange(ne, dtype=i32)[None, :]).astype(i32)
    counts = jnp.sum(onehot, axis=0)
    order = jnp.argsort(e_flat, stable=True).astype(i32)
    inv = jnp.argsort(order).astype(i32)
    tiles_e = (counts + tm - 1) // tm
    cum_tiles = jnp.cumsum(tiles_e)
    tile_start = cum_tiles - tiles_e
    grp_start = jnp.cumsum(counts) - counts
    nt = npair // tm + ne
    tile_ids = jnp.arange(nt, dtype=i32)
    tile_e = jnp.clip(jnp.searchsorted(cum_tiles, tile_ids, side="right"), 0, ne - 1).astype(i32)
    tile_off = (tile_ids - tile_start[tile_e]) * tm
    rows_in_tile = jnp.where(tile_ids < cum_tiles[-1], jnp.clip(counts[tile_e] - tile_off, 0, tm), 0)
    tile_ns = ((rows_in_tile + sub - 1) // sub).astype(i32)
    within = jnp.arange(tm, dtype=i32)[None, :]
    sorted_idx = jnp.clip((grp_start[tile_e] + tile_off)[:, None] + within, 0, npair - 1)
    src_tok = jnp.where(within < rows_in_tile[:, None], order[sorted_idx] // TOP_K, 0)
    pair_tile_start = jnp.sum(onehot * tile_start[None, :], axis=1)
    pair_grp_start = jnp.sum(onehot * grp_start[None, :], axis=1)
    pos = pair_tile_start * tm + (inv - pair_grp_start)

    y = _moe_experts(h, src_tok.reshape(nt, 1, tm).astype(i32), tile_e, tile_ns, layer,
                     w_up, b_up.reshape(b_up.shape[0], ne, 1, 2 * ff), w_down,
                     b_down.reshape(b_down.shape[0], ne, 1, d))
    pos_steps = pos.astype(i32).reshape(t // tt, tt, TOP_K).transpose(0, 2, 1).reshape(t // tt, 1, TOP_K * tt)
    return _moe_combine(y, pos_steps, x, topw, mods, m_idx, tp)


def _rms_kernel(x_ref, g_ref, o_ref):
    x = x_ref[...]
    o_ref[...] = x * lax.rsqrt(jnp.mean(x * x, axis=-1, keepdims=True) + EPS) * g_ref[...]


def _rmsnorm(x, g, tm=TOKEN_TILE):
    t, d = x.shape
    return pl.pallas_call(
        _rms_kernel,
        out_shape=jax.ShapeDtypeStruct((t, d), f32),
        grid=(t // tm,),
        in_specs=[pl.BlockSpec((tm, d), lambda i: (i, 0)), pl.BlockSpec((1, d), lambda i: (0, 0))],
        out_specs=pl.BlockSpec((tm, d), lambda i: (i, 0)),
        compiler_params=_cparams(("arbitrary",)),
        name="final_norm",
    )(x, g)


def _heads_prompt(a, tp):
    return a[:tp].reshape(tp, KV, G, HEAD_DIM).transpose(1, 2, 0, 3)


def _heads_sample(a, tp, ts, db):
    return a[tp:].reshape(ts, db, KV, G, HEAD_DIM).transpose(1, 2, 3, 0, 4).reshape(db, KV, G * ts, HEAD_DIM)


def _unheads(op, osm, tp, ts, db):
    a = op.transpose(2, 0, 1, 3).reshape(tp, N_HEADS * HEAD_DIM)
    b = osm.reshape(db, KV, G, ts, HEAD_DIM).transpose(3, 0, 1, 2, 4).reshape(ts * db, N_HEADS * HEAD_DIM)
    return jnp.concatenate([a, b], axis=0)


def _kv_prompt(a, tp):
    return a[:tp].reshape(tp, KV, HEAD_DIM).transpose(1, 0, 2)


def _kv_sample(a, tp, ts, db):
    return a[tp:].reshape(ts, db, KV, HEAD_DIM).transpose(1, 0, 2, 3)


def _pages_dim_major(rows, psz):
    n = rows.shape[0] // psz
    return rows.reshape(n, psz, -1, KV, HEAD_DIM).transpose(0, 2, 3, 4, 1)


def _block_expand(nb, nkeys, first_block=0):
    blk = first_block + jnp.arange(nkeys, dtype=i32) // BLK
    return (jnp.arange(nb, dtype=i32)[:, None] == blk[None, :]).astype(bf16)


def kernel(x_prompt, x_sample, state_swa, cache_nsa, state_nsa_win, page_table, c_prompt, c_sample,
           rel_table, w_qkv_a, sink_a, w_o_a, w_q_b, w_gate_b, w_o_b, w_kv_shared, cmp_pe, cmp_w1, cmp_w2,
           router_w, router_b, w_up, b_up, w_down, b_down, norm_g, ada_w, ada_b,
           kv_norm_g, ada_kv_w, ada_kv_b, final_norm_g):
    bsz, tp, d = x_prompt.shape
    db, ts, _ = x_sample.shape
    n_a = w_qkv_a.shape[0]
    depth = norm_g.shape[0]
    npg = page_table.shape[1]
    psz = cache_nsa.shape[1]
    past = npg * psz
    wb = state_swa.shape[2]
    nq = N_HEADS * HEAD_DIM
    nk = KV * HEAD_DIM
    scale = HEAD_DIM ** -0.5
    assert bsz == 1 and wb == WINDOW and state_nsa_win.shape[1] == WINDOW
    assert psz == 2 * BLK and past % BLK == 0 and ts <= BLK and tp % SLC_TQ == 0 and tp % TOKEN_TILE == 0
    t = tp + ts * db

    x = jnp.concatenate([x_prompt[0], x_sample.transpose(1, 0, 2).reshape(ts * db, d)], axis=0)
    cmat = jnp.concatenate([jnp.broadcast_to(c_prompt, (db, d)), c_sample], axis=0)
    mods = _ada(cmat, ada_w.reshape(depth * 2, d, 3 * d), ada_b.reshape(depth * 2, 1, 3 * d))
    mods_kv = _ada(cmat, ada_kv_w[None], ada_kv_b[None, None])

    lut = rel_table[_rel_bucket(jnp.arange(LUT_N))].astype(f32)
    c31 = lut[LUT_N - 1]
    bias_band = _toeplitz_bias(lut, WINDOW, 2 * WINDOW, WINDOW, window=WINDOW)
    bias_step = _toeplitz_bias(lut, ts, wb + ts, wb, window=WINDOW)

    swa_p, swa_s = [], []
    for layer in range(depth):
        if layer == n_a:
            h_kv = _modnorm(x, kv_norm_g.reshape(1, d), mods_kv, 0, tp)
            kvr = _mm(h_kv, w_kv_shared.astype(bf16))
            kv_s = kvr[tp:].reshape(ts, db, 6 * nk).transpose(1, 0, 2)
            nsa_rows_p = kvr[:tp, :4 * nk].reshape(1, tp, 4, KV, HEAD_DIM)
            nsa_rows_s = kv_s[:, :, :4 * nk].reshape(db, ts, 4, KV, HEAD_DIM)
            nsa_win_p = kvr[tp - WINDOW:tp, 4 * nk:].reshape(1, WINDOW, 2, KV, HEAD_DIM)
            nsa_win_s = jnp.concatenate(
                [state_nsa_win, kv_s[:, :, 4 * nk:].reshape(db, ts, 2, KV, HEAD_DIM)], axis=1)[:, -WINDOW:]

            nhid = cmp_w1.shape[-1]
            pe_t = jnp.tile(cmp_pe.transpose(0, 2, 1), (1, 1, 2)).reshape(2, HEAD_DIM, 1, psz)
            w1t = cmp_w1.reshape(2, BLK, HEAD_DIM // 2, 2, nhid).transpose(0, 2, 3, 1, 4)
            eye2 = jnp.eye(2, dtype=f32)
            w1p = (w1t[:, :, :, None, :, None, :] * eye2[None, None, None, :, None, :, None]).reshape(
                2, HEAD_DIM // 2, 2 * psz, 2 * nhid).astype(bf16)
            w2b = cmp_w2.astype(bf16)
            npg_p = tp // psz
            cmp_p = _compress(_pages_dim_major(kvr[:tp, :2 * nk], psz), 0,
                              jnp.arange(npg_p, dtype=i32)[None], pe_t, w1p, w2b)[0]
            pool = cache_nsa.transpose(0, 2, 3, 4, 1)
            cmp_s = _compress(pool, 0, page_table, pe_t, w1p, w2b)
            tail = jnp.pad(kv_s[:, :, :2 * nk], ((0, 0), (0, psz - ts), (0, 0))).reshape(db * psz, 2 * nk)
            cmp_t = _compress(_pages_dim_major(tail, psz), 0, jnp.arange(db, dtype=i32)[None], pe_t, w1p, w2b)[0]
            cmp_t = cmp_t.reshape(2, KV, db, 2, HEAD_DIM)[:, :, :, 0].transpose(2, 0, 1, 3)
            nb_s = past // BLK + 1
            nbp = ((nb_s + 255) // 256) * 256
            cmp_s = jnp.concatenate(
                [cmp_s, cmp_t[:, :, :, None], jnp.zeros((db, 2, KV, nbp - nb_s, HEAD_DIM), f32)], axis=3)
            kc_p, vc_p = cmp_p[0].astype(bf16), cmp_p[1].astype(bf16)
            kc_s, vc_s = cmp_s[:, 0].astype(bf16), cmp_s[:, 1].astype(bf16)

        g_mix = norm_g[layer, 0].reshape(1, d)
        h = _modnorm(x, g_mix, mods, 2 * layer, tp)
        if layer < n_a:
            y = _mm(h, w_qkv_a[layer].astype(bf16))
            q, k, v = y[:, :nq], y[:, nq:nq + nk], y[:, nq + nk:]
            sink = sink_a[layer].astype(f32)
            o_p = _band_attn(_heads_prompt(q, tp).astype(bf16), _kv_prompt(k, tp).astype(bf16),
                             _kv_prompt(v, tp).astype(bf16), bias_band, _head_rows(sink, WINDOW))
            k_s, v_s = _kv_sample(k, tp, ts, db), _kv_sample(v, tp, ts, db)
            kk = jnp.concatenate([state_swa[layer, :, :, 0], k_s], axis=1).transpose(0, 2, 1, 3)
            vv = jnp.concatenate([state_swa[layer, :, :, 1], v_s], axis=1).transpose(0, 2, 1, 3)
            o_s = _step_attn(_heads_sample(q, tp, ts, db).astype(bf16), kk.astype(bf16), vv.astype(bf16),
                             bias_step, _head_rows(sink, ts))
            swa_p.append(jnp.stack([k[tp - wb:tp].reshape(1, wb, KV, HEAD_DIM),
                                    v[tp - wb:tp].reshape(1, wb, KV, HEAD_DIM)], axis=2))
            swa_s.append(jnp.concatenate([state_swa[layer], jnp.stack([k_s, v_s], axis=2)], axis=1)[:, -wb:])
            x = _mm_res(_unheads(o_p, o_s, tp, ts, db), w_o_a[layer].astype(bf16), x, mods, 2 * layer, tp)
        else:
            j = layer - n_a
            qf = _mm(h, w_q_b[j].astype(bf16))
            ngate = 3 * N_HEADS
            glog = _mm(h, jnp.pad(w_gate_b[j], ((0, 0), (0, LANES - ngate))).astype(bf16))
            q_p = _heads_prompt(qf, tp).astype(bf16)
            q_s = _heads_sample(qf, tp, ts, db).astype(bf16)

            tq = WINDOW
            iw = jnp.arange(tq)
            jj = jnp.arange(tq // BLK + 2)
            dist_c = iw[:, None] - (BLK * (jj[None, :] - 2) + BLK - 1)
            bc = _dist_bias(lut, dist_c, dist_c >= 0)
            bc = bc.transpose(0, 2, 1)[..., None]
            oc_p, sel_p = _cmp_prompt(q_p, kc_p, vc_p, bc, _head_rows(c31, tq))
            dist_cs = (past + jnp.arange(ts))[:, None] - (BLK * jnp.arange(nbp)[None, :] + BLK - 1)
            bias_cs = _dist_bias(lut, dist_cs, dist_cs >= 0)
            cur_s = ((past + jnp.arange(ts)) // BLK).astype(i32).reshape(ts, 1)
            oc_s, sel_s = _cmp_sample(q_s, kc_s, vc_s, bias_cs, cur_s)

            tqs = SLC_TQ
            bias_sl = _toeplitz_bias(lut, tqs, 2 * tqs, tqs, sub=c31)
            kst_p = _kv_prompt(kvr[:, 2 * nk:3 * nk], tp).transpose(0, 2, 1).astype(bf16)
            vs_p = _kv_prompt(kvr[:, 3 * nk:4 * nk], tp)
            vs1_p = jnp.concatenate([vs_p, jnp.ones((KV, tp, 1), f32),
                                     jnp.zeros((KV, tp, LANES - HEAD_DIM - 1), f32)], axis=2).astype(bf16)
            os_p = _slc_prompt((_heads_prompt(qf, tp) * scale).astype(bf16), kst_p, vs1_p, sel_p,
                               _block_expand(sel_p.shape[-1], tp), bias_sl, tqs)

            rows = KV * G * ts
            eye = jnp.eye(KV, dtype=f32)
            qs_scaled = _heads_sample(qf, tp, ts, db) * scale
            qbd = (qs_scaled[:, :, :, None, :] * eye[None, :, None, :, None]).reshape(db, rows, nk).astype(bf16)
            selr = jnp.broadcast_to(sel_s[:, :, None], (db, KV, G, ts, nbp)).reshape(db, rows, nbp)
            ppt = min(SLC_PAGES_PER_TILE, npg)
            tk = ppt * psz
            bias_l = _toeplitz_bias(lut, ts, tk, tk, sub=c31).reshape(rows, tk)
            bias_n = _toeplitz_bias(lut, ts, BLK, 0, sub=c31).reshape(rows, BLK)
            new_t = jnp.pad(kv_s[:, :, 2 * nk:4 * nk], ((0, 0), (0, BLK - ts), (0, 0))).transpose(0, 2, 1)
            knew, vnew = new_t[:, :nk].astype(bf16), new_t[:, nk:].astype(bf16)
            os_s = _slc_sample(pool.reshape(pool.shape[0], 4, nk, psz), 2, page_table, qbd, selr,
                               _block_expand(nbp, past), knew, vnew, bias_l, bias_n,
                               _block_expand(nbp, BLK, past // BLK), ppt)
            os_s = os_s.reshape(db, KV, G * ts, HEAD_DIM)

            kw_p = _kv_prompt(kvr[:, 4 * nk:5 * nk], tp).astype(bf16)
            vw_p = _kv_prompt(kvr[:, 5 * nk:6 * nk], tp).astype(bf16)
            ow_p = _band_attn(q_p, kw_p, vw_p, bias_band)
            kw_s = kv_s[:, :, 4 * nk:5 * nk].reshape(db, ts, KV, HEAD_DIM)
            vw_s = kv_s[:, :, 5 * nk:6 * nk].reshape(db, ts, KV, HEAD_DIM)
            kk = jnp.concatenate([state_nsa_win[:, :, 0], kw_s], axis=1).transpose(0, 2, 1, 3)
            vv = jnp.concatenate([state_nsa_win[:, :, 1], vw_s], axis=1).transpose(0, 2, 1, 3)
            ow_s = _step_attn(q_s, kk.astype(bf16), vv.astype(bf16), bias_step)

            head_of_lane = jnp.arange(nq) // HEAD_DIM
            expand = (jnp.arange(LANES)[None, :, None]
                      == (jnp.arange(3)[:, None, None] * N_HEADS + head_of_lane[None, None, :])).astype(bf16)
            merged = _merge(glog, _unheads(oc_p, oc_s, tp, ts, db), _unheads(os_p, os_s, tp, ts, db),
                            _unheads(ow_p, ow_s, tp, ts, db), expand)
            x = _mm_res(merged, w_o_b[j].astype(bf16), x, mods, 2 * layer, tp)

        x = _moe_layer(x, norm_g[layer, 1].reshape(1, d), mods, 2 * layer + 1, tp, layer,
                       router_w[layer], router_b[layer], w_up, b_up, w_down, b_down)

    y = _rmsnorm(x, final_norm_g.reshape(1, d))
    y_prompt = y[:tp].reshape(1, tp, d)
    y_sample = y[tp:].reshape(ts, db, d).transpose(1, 0, 2)
    return (y_prompt, y_sample, jnp.stack(swa_p), jnp.stack(swa_s), nsa_rows_p, nsa_rows_s, nsa_win_p, nsa_win_s)
```

```python
import functools
import math

import jax
import jax.numpy as jnp
from jax import lax
from jax.experimental import pallas as pl
from jax.experimental.pallas import tpu as pltpu

f32 = jnp.float32
bf16 = jnp.bfloat16
i32 = jnp.int32

N_HEADS = 32
HEAD_DIM = 64
KV = 4
G = N_HEADS // KV
WINDOW = 128
BLK = 64
N_SEL = 16
TOP_K = 4
SWIGLU_LIMIT = 7.0
SWIGLU_ALPHA = 1.702
REL_BUCKETS = 32
REL_MAX_DIST = 128
EPS = 1e-6
NEG = -1e30
TINY = 1e-20
FORCE_SCORE = 1e4
LUT_N = 256
LANES = 128
VMEM_LIMIT = 56 * 1024 * 1024

TOKEN_TILE = 256
SLC_TQ = 256
SLC_ROWS = 128
SLC_PAGES_PER_TILE = 8
MOE_TM = 1280
MOE_SUB = 256
MOE_FC = 256
MOE_TT = 64


def _cparams(sem, vmem=None):
    return pltpu.CompilerParams(dimension_semantics=sem, vmem_limit_bytes=vmem)


def _col_tile(n, cap=512):
    tn = cap
    while n % tn:
        tn //= 2
    assert tn % LANES == 0
    return tn


def _dot(a, b):
    return jnp.dot(a, b, preferred_element_type=f32)


def _dot_nt(a, b):
    return lax.dot_general(a, b, (((1,), (1,)), ((), ())), preferred_element_type=f32)


def _rel_bucket(dist):
    n = jnp.maximum(dist, 0)
    max_exact = REL_BUCKETS // 2
    nf = jnp.maximum(n, 1).astype(f32)
    large = max_exact + (jnp.log(nf / max_exact) / math.log(REL_MAX_DIST / max_exact)
                         * (REL_BUCKETS - max_exact)).astype(i32)
    return jnp.where(n < max_exact, n, jnp.minimum(large, REL_BUCKETS - 1))


def _dist_bias(lut, dist, ok):
    b = lut[jnp.clip(dist, 0, LUT_N - 1)]
    b = jnp.where(ok[..., None], b, NEG)
    r, c = dist.shape
    return b.reshape(r, c, KV, G).transpose(2, 3, 0, 1).reshape(KV, G * r, c)


def _toeplitz_bias(lut, nrows, ncols, c, window=None, sub=None):
    lw = nrows + ncols
    dist = c + nrows - 1 - jnp.arange(lw)
    ok = dist >= 0
    if window is not None:
        ok = ok & (dist < window)
    vals = lut[jnp.clip(dist, 0, LUT_N - 1)]
    if sub is not None:
        vals = vals - sub[None, :]
    w = jnp.where(ok[:, None], vals, NEG).T
    skew = jnp.tile(w, (1, nrows))[:, :nrows * (lw - 1)].reshape(-1, nrows, lw - 1)
    return skew[:, :, nrows - 1:nrows - 1 + ncols].reshape(KV, G * nrows, ncols)


def _head_rows(vec, r):
    return jnp.broadcast_to(vec.reshape(KV, G, 1, 1), (KV, G, r, 1)).reshape(KV, G * r, 1).astype(f32)


def _ada_kernel(c_ref, w_ref, b_ref, o_ref):
    a = jax.nn.silu(c_ref[...]).astype(bf16)
    o_ref[...] = _dot(a, w_ref[...].astype(bf16)) + b_ref[...]


def _ada(cmat, w, b):
    nm, d, n = w.shape
    r = cmat.shape[0]
    tn = _col_tile(n)
    return pl.pallas_call(
        _ada_kernel,
        out_shape=jax.ShapeDtypeStruct((nm, r, n), f32),
        grid=(nm, n // tn),
        in_specs=[pl.BlockSpec((r, d), lambda m, j: (0, 0)),
                  pl.BlockSpec((None, d, tn), lambda m, j: (m, 0, j)),
                  pl.BlockSpec((None, 1, tn), lambda m, j: (m, 0, j))],
        out_specs=pl.BlockSpec((None, r, tn), lambda m, j: (m, 0, j)),
        compiler_params=_cparams(("arbitrary", "arbitrary")),
        name="ada_mods",
    )(cmat, w, b)


def _modulated(x_ref, g_ref, sh_ref, sc_ref):
    x = x_ref[...]
    tm, d = x.shape
    rg = sh_ref.shape[0]
    y = x * lax.rsqrt(jnp.mean(x * x, axis=-1, keepdims=True) + EPS) * g_ref[...]
    y = y.reshape(tm // rg, rg, d) * (1.0 + sc_ref[...])[None] + sh_ref[...][None]
    return y.reshape(tm, d)


def _modnorm_kernel(x_ref, g_ref, sh_ref, sc_ref, h_ref):
    h_ref[...] = _modulated(x_ref, g_ref, sh_ref, sc_ref).astype(bf16)


def _router_kernel(x_ref, g_ref, sh_ref, sc_ref, whi_ref, wlo_ref, rb_ref, h_ref, ti_ref, tw_ref):
    h = _modulated(x_ref, g_ref, sh_ref, sc_ref)
    hi = h.astype(bf16)
    lo = (h - hi.astype(f32)).astype(bf16)
    half = h.shape[1] // 2
    w_lo = lax.bitcast_convert_type(hi[:, :half].astype(f32), jnp.uint32)
    w_hi = lax.bitcast_convert_type(hi[:, half:].astype(f32), jnp.uint32)
    h_ref[...] = w_hi | lax.shift_right_logical(w_lo, jnp.uint32(16))
    logits = _dot(hi, whi_ref[...]) + (_dot(lo, whi_ref[...]) + _dot(hi, wlo_ref[...])) + rb_ref[...]
    lane = lax.broadcasted_iota(i32, logits.shape, 1).astype(f32)
    ti = jnp.zeros(logits.shape, f32)
    tv = jnp.zeros(logits.shape, f32)
    v0 = None
    den = None
    for k in range(TOP_K):
        m = jnp.max(logits, axis=-1, keepdims=True)
        idx = jnp.min(jnp.where(logits == m, lane, float(LANES)), axis=-1, keepdims=True)
        if k == 0:
            v0 = m
            e = jnp.ones_like(m)
            den = e
        else:
            e = jnp.exp(m - v0)
            den = den + e
        ti = jnp.where(lane == k, idx, ti)
        tv = jnp.where(lane == k, e, tv)
        logits = jnp.where(lane == idx, -jnp.inf, logits)
    ti_ref[...] = ti.astype(i32)
    tw_ref[...] = tv / den


def _mod_specs(d, rg, n_prompt_tiles, m_idx, cols):
    def spec(col):
        return pl.BlockSpec((None, rg, d), lambda i: (m_idx, jnp.where(i < n_prompt_tiles, 0, 1), col))
    return [spec(c) for c in cols]


def _modnorm(x, g, mods, m_idx, tp, router=None, tm=TOKEN_TILE):
    t, d = x.shape
    rg = mods.shape[1] // 2
    grid = (t // tm,)
    in_specs = [pl.BlockSpec((tm, d), lambda i: (i, 0)),
                pl.BlockSpec((1, d), lambda i: (0, 0))] + _mod_specs(d, rg, tp // tm, m_idx, (0, 1))
    h_spec = pl.BlockSpec((tm, d), lambda i: (i, 0))
    if router is None:
        return pl.pallas_call(
            _modnorm_kernel, out_shape=jax.ShapeDtypeStruct((t, d), bf16), grid=grid, in_specs=in_specs,
            out_specs=h_spec, compiler_params=_cparams(("arbitrary",)), name="modnorm",
        )(x, g, mods, mods)
    whi, wlo, rb = router
    in_specs += [pl.BlockSpec((d, LANES), lambda i: (0, 0)),
                 pl.BlockSpec((d, LANES), lambda i: (0, 0)),
                 pl.BlockSpec((1, LANES), lambda i: (0, 0))]
    lane_spec = pl.BlockSpec((tm, LANES), lambda i: (i, 0))
    return pl.pallas_call(
        _router_kernel,
        out_shape=(jax.ShapeDtypeStruct((t, d // 2), jnp.uint32), jax.ShapeDtypeStruct((t, LANES), i32),
                   jax.ShapeDtypeStruct((t, LANES), f32)),
        grid=grid, in_specs=in_specs,
        out_specs=(pl.BlockSpec((tm, d // 2), lambda i: (i, 0)), lane_spec, lane_spec),
        compiler_params=_cparams(("arbitrary",)), name="modnorm_router",
    )(x, g, mods, mods, whi, wlo, rb)


def _mm_kernel(a_ref, w_ref, o_ref):
    o_ref[...] = _dot(a_ref[...].astype(bf16), w_ref[...])


def _mm_res_kernel(a_ref, w_ref, x_ref, gt_ref, o_ref):
    acc = _dot(a_ref[...].astype(bf16), w_ref[...])
    tm, tn = acc.shape
    rg = gt_ref.shape[0]
    upd = acc.reshape(tm // rg, rg, tn) * gt_ref[...][None]
    o_ref[...] = x_ref[...] + upd.reshape(tm, tn)


def _mm(a, w, tm=TOKEN_TILE, tn=512):
    t, k = a.shape
    n = w.shape[1]
    tn = _col_tile(n, tn)
    return pl.pallas_call(
        _mm_kernel,
        out_shape=jax.ShapeDtypeStruct((t, n), f32),
        grid=(n // tn, t // tm),
        in_specs=[pl.BlockSpec((tm, k), lambda j, i: (i, 0)),
                  pl.BlockSpec((k, tn), lambda j, i: (0, j))],
        out_specs=pl.BlockSpec((tm, tn), lambda j, i: (i, j)),
        compiler_params=_cparams(("arbitrary", "arbitrary")),
        name="matmul",
    )(a, w)


def _mm_res(a, w, x, mods, m_idx, tp, tm=TOKEN_TILE, tn=512):
    t, k = a.shape
    n = w.shape[1]
    tn = _col_tile(n, tn)
    rg = mods.shape[1] // 2
    npt = tp // tm
    nj = n // tn
    return pl.pallas_call(
        _mm_res_kernel,
        out_shape=jax.ShapeDtypeStruct((t, n), f32),
        grid=(nj, t // tm),
        in_specs=[pl.BlockSpec((tm, k), lambda j, i: (i, 0)),
                  pl.BlockSpec((k, tn), lambda j, i: (0, j)),
                  pl.BlockSpec((tm, tn), lambda j, i: (i, j)),
                  pl.BlockSpec((None, rg, tn), lambda j, i: (m_idx, jnp.where(i < npt, 0, 1), 2 * nj + j))],
        out_specs=pl.BlockSpec((tm, tn), lambda j, i: (i, j)),
        compiler_params=_cparams(("arbitrary", "arbitrary")),
        name="matmul_residual",
    )(a, w, x, mods)


def _band_kernel(q_ref, kp_ref, kc_ref, vp_ref, vc_ref, b_ref, o_ref):
    g, tq, hd = q_ref.shape
    sink_keys = b_ref.shape[1] - 2 * tq
    has_sink = sink_keys > 0
    q = q_ref[...].reshape(g * tq, hd)
    ks = [kp_ref[...], kc_ref[...]]
    vs = [vp_ref[...], vc_ref[...]]
    if has_sink:
        ks.append(jnp.zeros((sink_keys, hd), bf16))
        vs.append(jnp.zeros((sink_keys, hd), bf16))
    kk = jnp.concatenate(ks, axis=0)
    vv = jnp.concatenate(vs, axis=0)
    s = _dot_nt(q, kk) * (hd ** -0.5) + b_ref[...]
    col = lax.broadcasted_iota(i32, s.shape, 1)
    first = pl.program_id(1) == 0
    s = jnp.where(jnp.logical_and(first, col < tq), NEG, s)
    m = jnp.max(s, axis=-1, keepdims=True)
    e = jnp.exp(s - m)
    den = jnp.sum(e, axis=-1, keepdims=True)
    o = _dot(e.astype(bf16), vv) / den
    o_ref[...] = o.reshape(g, tq, hd)


def _band_attn(q, k, v, bias, sink=None):
    _, g, t, hd = q.shape
    w = WINDOW
    if sink is not None:
        sink_cols = jnp.concatenate([_head_rows(sink, w), jnp.full((KV, g * w, LANES - 1), NEG, f32)], axis=2)
        bias = jnp.concatenate([bias, sink_cols], axis=2)
    nkeys = bias.shape[2]
    kspec_p = pl.BlockSpec((None, w, hd), lambda h, i: (h, jnp.maximum(i - 1, 0), 0))
    kspec_c = pl.BlockSpec((None, w, hd), lambda h, i: (h, i, 0))
    return pl.pallas_call(
        _band_kernel,
        out_shape=jax.ShapeDtypeStruct((KV, g, t, hd), f32),
        grid=(KV, t // w),
        in_specs=[pl.BlockSpec((None, g, w, hd), lambda h, i: (h, 0, i, 0)),
                  kspec_p, kspec_c, kspec_p, kspec_c,
                  pl.BlockSpec((None, g * w, nkeys), lambda h, i: (h, 0, 0))],
        out_specs=pl.BlockSpec((None, g, w, hd), lambda h, i: (h, 0, i, 0)),
        compiler_params=_cparams(("arbitrary", "arbitrary")),
        name="band_attn",
    )(q, k, k, v, v, bias)


def _step_kernel(q_ref, k_ref, v_ref, b_ref, *rest, has_sink):
    if has_sink:
        s_ref, o_ref = rest
    else:
        (o_ref,) = rest
    hd = q_ref.shape[-1]
    for h in range(KV):
        s = _dot_nt(q_ref[h], k_ref[h]) * (hd ** -0.5) + b_ref[h]
        m = jnp.max(s, axis=-1, keepdims=True)
        if has_sink:
            m = jnp.maximum(m, s_ref[h])
        e = jnp.exp(s - m)
        den = jnp.sum(e, axis=-1, keepdims=True)
        if has_sink:
            den = den + jnp.exp(s_ref[h] - m)
        o_ref[h] = _dot(e.astype(bf16), v_ref[h]) / den


def _step_attn(q, k, v, bias, sink=None):
    db, _, r, hd = q.shape
    l = k.shape[2]
    in_specs = [pl.BlockSpec((None, KV, r, hd), lambda b: (b, 0, 0, 0)),
                pl.BlockSpec((None, KV, l, hd), lambda b: (b, 0, 0, 0)),
                pl.BlockSpec((None, KV, l, hd), lambda b: (b, 0, 0, 0)),
                pl.BlockSpec((KV, r, l), lambda b: (0, 0, 0))]
    args = [q, k, v, bias]
    if sink is not None:
        in_specs.append(pl.BlockSpec((KV, r, 1), lambda b: (0, 0, 0)))
        args.append(sink)
    return pl.pallas_call(
        functools.partial(_step_kernel, has_sink=sink is not None),
        out_shape=jax.ShapeDtypeStruct((db, KV, r, hd), f32),
        grid=(db,),
        in_specs=in_specs,
        out_specs=pl.BlockSpec((None, KV, r, hd), lambda b: (b, 0, 0, 0)),
        compiler_params=_cparams(("arbitrary",)),
        name="step_attn",
    )(*args)


def _compress_kernel(pt_ref, pool_ref, pe_ref, w1_hbm, w2_ref, o_ref, buf, sem, w1_s, wsem, acc,
                     *, ch0, pgs, nchunk):
    b = pl.program_id(0)
    c = pl.program_id(1)
    step = b * nchunk + c
    nsteps = pl.num_programs(0) * nchunk
    slot = step % 2
    hd = buf.shape[3]
    nhid = w2_ref.shape[1]

    def page_copy(bb, cc, p, sl):
        page = pt_ref[bb, cc * pgs + p]
        return pltpu.make_async_copy(pool_ref.at[page, pl.ds(ch0, 2)], buf.at[sl, :, :, :, p, :], sem.at[sl])

    def start(st, sl):
        bb = st // nchunk
        cc = st % nchunk

        def body(p, carry):
            page_copy(bb, cc, p, sl).start()
            return carry
        lax.fori_loop(0, pgs, body, 0)

    @pl.when(step == 0)
    def _():
        w1_copy = pltpu.make_async_copy(w1_hbm, w1_s, wsem.at[0])
        w1_copy.start()
        start(step, slot)
        w1_copy.wait()

    @pl.when(step + 1 < nsteps)
    def _():
        start(step + 1, 1 - slot)

    def wait_body(p, carry):
        page_copy(b, c, p, slot).wait()
        return carry
    lax.fori_loop(0, pgs, wait_body, 0)

    for ch in range(2):
        for dp in range(hd // 2):
            parts = []
            for h in range(KV):
                pair = [(buf[slot, ch, h, 2 * dp + dd] + pe_ref[ch, 2 * dp + dd]).astype(bf16) for dd in range(2)]
                parts.append(jnp.concatenate(pair, axis=1))
            part = _dot(jnp.concatenate(parts, axis=0), w1_s[ch, dp])
            if dp == 0:
                acc[...] = part
            else:
                acc[...] += part
        hid = jax.nn.gelu(acc[...])
        for n in range(2):
            out = _dot(hid[:, n * nhid:(n + 1) * nhid].astype(bf16), w2_ref[ch])
            o_ref[ch, n] = out.reshape(KV, pgs, hd)


def _compress(pool, ch0, pt, pe_t, w1p, w2):
    nseq, npg = pt.shape
    psz = pool.shape[-1]
    pgs = min(npg, 64)
    nchunk = npg // pgs
    nhid = w2.shape[1]
    kern = functools.partial(_compress_kernel, ch0=ch0, pgs=pgs, nchunk=nchunk)
    out = pl.pallas_call(
        kern,
        out_shape=jax.ShapeDtypeStruct((nseq, 2, 2, KV, npg, HEAD_DIM), f32),
        grid_spec=pltpu.PrefetchScalarGridSpec(
            num_scalar_prefetch=1,
            grid=(nseq, nchunk),
            in_specs=[pl.BlockSpec(memory_space=pl.ANY),
                      pl.BlockSpec(pe_t.shape, lambda b, c, pt: (0, 0, 0, 0)),
                      pl.BlockSpec(memory_space=pl.ANY),
                      pl.BlockSpec(w2.shape, lambda b, c, pt: (0, 0, 0))],
            out_specs=pl.BlockSpec((None, 2, 2, KV, pgs, HEAD_DIM), lambda b, c, pt: (b, 0, 0, 0, c, 0)),
            scratch_shapes=[pltpu.VMEM((2, 2, KV, HEAD_DIM, pgs, psz), f32),
                            pltpu.SemaphoreType.DMA((2,)),
                            pltpu.VMEM(w1p.shape, bf16),
                            pltpu.SemaphoreType.DMA((1,)),
                            pltpu.VMEM((KV * pgs, 2 * nhid), f32)]),
        compiler_params=_cparams(("arbitrary", "arbitrary"), VMEM_LIMIT),
        name="compress",
    )(pt, pool, pe_t, w1p, w2)
    return out.transpose(0, 1, 3, 4, 2, 5).reshape(nseq, 2, KV, 2 * npg, HEAD_DIM)


def _cmp_core(q, kc, vc, bias, cur, g, tq):
    hd = q.shape[-1]
    nb = kc.shape[0]
    vis = bias > 0.5 * NEG
    s = _dot_nt(q, kc) * (hd ** -0.5) + bias
    m = jnp.max(s, axis=-1, keepdims=True)
    e = jnp.where(vis, jnp.exp(s - m), 0.0)
    p = e / jnp.maximum(jnp.sum(e, axis=-1, keepdims=True), TINY)
    o = _dot(p.astype(bf16), vc)
    imp = jnp.sum(p.reshape(g, tq, nb), axis=0)
    col = lax.broadcasted_iota(i32, (tq, nb), 1)
    colf = col.astype(f32)
    forced = (col == 0) | (col == cur) | (col == cur - 1)
    imp = jnp.where(forced, FORCE_SCORE, imp)
    imp = jnp.where(col <= cur, imp, NEG)
    sel = jnp.zeros((tq, nb), f32)
    for _ in range(N_SEL):
        mx = jnp.max(imp, axis=-1, keepdims=True)
        idx = jnp.min(jnp.where(imp == mx, colf, float(nb)), axis=-1, keepdims=True)
        hit = colf == idx
        sel = jnp.where(hit, 1.0, sel)
        imp = jnp.where(hit, -jnp.inf, imp)
    return o, sel


def _cmp_prompt_kernel(qt_ref, kc_ref, vct_ref, bc_ref, c31_ref, o_ref, sel_ref, *, g, tq):
    hd, cols = qt_ref.shape
    nb = kc_ref.shape[0]
    qi = pl.program_id(1)
    blk = lax.broadcasted_iota(i32, (nb, cols), 0)
    first = (tq // BLK) * qi - 2
    bias = jnp.where(blk < first, c31_ref[...], NEG)
    for jj in range(bc_ref.shape[0]):
        bias = jnp.where(blk == first + jj, bc_ref[jj], bias)
    vis = bias > 0.5 * NEG
    s = _dot(kc_ref[...], qt_ref[...]) * (hd ** -0.5) + bias
    m = jnp.max(s, axis=0, keepdims=True)
    e = jnp.where(vis, jnp.exp(s - m), 0.0)
    p = e / jnp.maximum(jnp.sum(e, axis=0, keepdims=True), TINY)
    o_ref[...] = _dot(vct_ref[...], p.astype(bf16))

    imp = p[:, :tq]
    for gi in range(1, g):
        imp = imp + p[:, gi * tq:(gi + 1) * tq]
    blk = lax.broadcasted_iota(i32, (nb, tq), 0)
    blkf = blk.astype(f32)
    cur = (qi * tq + lax.broadcasted_iota(i32, (1, tq), 1)) // BLK
    forced = (blk == 0) | (blk == cur) | (blk == cur - 1)
    imp = jnp.where(forced, FORCE_SCORE, imp)
    imp = jnp.where(blk <= cur, imp, NEG)
    sel = jnp.zeros((nb, tq), f32)
    for _ in range(N_SEL):
        mx = jnp.max(imp, axis=0, keepdims=True)
        idx = jnp.min(jnp.where(imp == mx, blkf, float(nb)), axis=0, keepdims=True)
        hit = blkf == idx
        sel = jnp.where(hit, 1.0, sel)
        imp = jnp.where(hit, -jnp.inf, imp)
    sel_ref[...] = sel.astype(bf16)


def _cmp_prompt(qt, kc, vct, bc, c31, g, tq):
    _, nqt, hd, cols = qt.shape
    nb = kc.shape[1]
    nband = bc.shape[1]
    return pl.pallas_call(
        functools.partial(_cmp_prompt_kernel, g=g, tq=tq),
        out_shape=(jax.ShapeDtypeStruct((KV, nqt, hd, cols), f32),
                   jax.ShapeDtypeStruct((KV, nb, nqt * tq), bf16)),
        grid=(KV, nqt),
        in_specs=[pl.BlockSpec((None, None, hd, cols), lambda h, i: (h, i, 0, 0)),
                  pl.BlockSpec((None, nb, hd), lambda h, i: (h, 0, 0)),
                  pl.BlockSpec((None, hd, nb), lambda h, i: (h, 0, 0)),
                  pl.BlockSpec((None, nband, 1, cols), lambda h, i: (h, 0, 0, 0)),
                  pl.BlockSpec((None, 1, cols), lambda h, i: (h, 0, 0))],
        out_specs=(pl.BlockSpec((None, None, hd, cols), lambda h, i: (h, i, 0, 0)),
                   pl.BlockSpec((None, nb, tq), lambda h, i: (h, 0, i))),
        compiler_params=_cparams(("arbitrary", "arbitrary")),
        name="cmp_prompt",
    )(qt, kc, vct, bc, c31)


def _cmp_sample_kernel(q_ref, kc_ref, vc_ref, b_ref, cur_ref, o_ref, sel_ref):
    ts = cur_ref.shape[0]
    g = q_ref.shape[1] // ts
    for h in range(KV):
        o, sel = _cmp_core(q_ref[h], kc_ref[h], vc_ref[h], b_ref[h], cur_ref[...], g, ts)
        o_ref[h] = o
        sel_ref[h] = sel.astype(bf16)


def _cmp_sample(q, kc, vc, bias, cur):
    db, _, r, hd = q.shape
    nbp = kc.shape[2]
    ts = cur.shape[0]
    return pl.pallas_call(
        _cmp_sample_kernel,
        out_shape=(jax.ShapeDtypeStruct((db, KV, r, hd), f32), jax.ShapeDtypeStruct((db, KV, ts, nbp), bf16)),
        grid=(db,),
        in_specs=[pl.BlockSpec((None, KV, r, hd), lambda b: (b, 0, 0, 0)),
                  pl.BlockSpec((None, KV, nbp, hd), lambda b: (b, 0, 0, 0)),
                  pl.BlockSpec((None, KV, nbp, hd), lambda b: (b, 0, 0, 0)),
                  pl.BlockSpec((KV, r, nbp), lambda b: (0, 0, 0)),
                  pl.BlockSpec((ts, 1), lambda b: (0, 0))],
        out_specs=(pl.BlockSpec((None, KV, r, hd), lambda b: (b, 0, 0, 0)),
                   pl.BlockSpec((None, KV, ts, nbp), lambda b: (b, 0, 0, 0))),
        compiler_params=_cparams(("arbitrary",)),
        name="cmp_sample",
    )(q, kc, vc, bias, cur)


def _softmax_chunk(a, vt_dot, m_ref, l_ref, acc_ref, rows):
    m_old = m_ref[rows, :]
    m_new = jnp.maximum(m_old, jnp.max(a, axis=-1, keepdims=True))
    alpha = jnp.exp(m_old - m_new)
    e = jnp.exp(a - m_new)
    l_ref[rows, :] = alpha * l_ref[rows, :] + jnp.sum(e, axis=-1, keepdims=True)
    acc_ref[rows, :] = alpha * acc_ref[rows, :] + vt_dot(e.astype(bf16))
    m_ref[rows, :] = m_new


def _slc_prompt_kernel(q_ref, kt_ref, v_ref, sel_ref, ex_ref, b_ref, o_ref,
                       s_scr, p_scr, madd_scr, m_scr, mt_scr, al_scr, acc_scr):
    g, tq, hd = q_ref.shape
    qi = pl.program_id(1)
    rows = g * tq
    rc = SLC_ROWS
    nchunk = rows // rc
    per_head = tq // rc

    def tile(kstart, width, bias_col):
        kt = kt_ref[:, pl.ds(kstart, width)]
        vt = v_ref[pl.ds(kstart, width), :]
        s_scr[:, :width] = _dot(q_ref[...].reshape(rows, hd), kt)
        msk = _dot(sel_ref[...], ex_ref[:, pl.ds(kstart, width)])
        madd_scr[:, :width] = jnp.where(msk > 0.5, 0.0, NEG)

        def logits(ci):
            part = ci % per_head
            a = s_scr[ci * rc:(ci + 1) * rc, :width] + madd_scr[part * rc:(part + 1) * rc, :width]
            if bias_col is not None:
                a = a + b_ref[ci * rc:(ci + 1) * rc, bias_col:bias_col + width]
            return a

        for ci in range(nchunk):
            mt_scr[ci * rc:(ci + 1) * rc, :] = jnp.broadcast_to(
                jnp.max(logits(ci), axis=-1, keepdims=True), (rc, LANES))
        m_old = m_scr[...]
        m_new = jnp.maximum(m_old, mt_scr[...])
        al_scr[...] = jnp.exp(m_old - m_new)
        m_scr[...] = m_new
        for ci in range(nchunk):
            m_c = m_scr[ci * rc:(ci + 1) * rc, :]
            e = jnp.exp(logits(ci) - jnp.tile(m_c, (1, width // LANES)))
            p_scr[ci * rc:(ci + 1) * rc, :width] = e.astype(bf16)
        acc_scr[...] = al_scr[...] * acc_scr[...] + _dot(p_scr[:, :width], vt)

    m_scr[...] = jnp.full(m_scr.shape, NEG, f32)
    acc_scr[...] = jnp.zeros(acc_scr.shape, f32)

    @pl.when(qi == 0)
    def _():
        tile(0, tq, tq)

    @pl.when(qi > 0)
    def _():
        tile(pl.multiple_of((qi - 1) * tq, tq), 2 * tq, 0)

    nfar = jnp.maximum(qi - 1, 0)

    def far(j, carry):
        tile(pl.multiple_of(j * 2 * tq, 2 * tq), 2 * tq, None)
        return carry
    lax.fori_loop(0, nfar // 2, far, 0)

    @pl.when(nfar % 2 == 1)
    def _():
        tile(pl.multiple_of((nfar - 1) * tq, tq), tq, None)

    acc = acc_scr[...]
    o_ref[...] = (acc[:, :hd] / acc[:, hd:hd + 1]).reshape(g, tq, hd)


def _slc_prompt(q, kt, v1, sel, expand, bias, tq):
    _, g, t, hd = q.shape
    nb = sel.shape[-1]
    rows = g * tq
    return pl.pallas_call(
        _slc_prompt_kernel,
        out_shape=jax.ShapeDtypeStruct((KV, g, t, hd), f32),
        grid=(KV, t // tq),
        in_specs=[pl.BlockSpec((None, g, tq, hd), lambda h, i: (h, 0, i, 0)),
                  pl.BlockSpec((None, hd, t), lambda h, i: (h, 0, 0)),
                  pl.BlockSpec((None, t, LANES), lambda h, i: (h, 0, 0)),
                  pl.BlockSpec((None, tq, nb), lambda h, i: (h, i, 0)),
                  pl.BlockSpec((nb, t), lambda h, i: (0, 0)),
                  pl.BlockSpec((None, rows, 2 * tq), lambda h, i: (h, 0, 0))],
        out_specs=pl.BlockSpec((None, g, tq, hd), lambda h, i: (h, 0, i, 0)),
        scratch_shapes=[pltpu.VMEM((rows, 2 * tq), f32), pltpu.VMEM((rows, 2 * tq), bf16),
                        pltpu.VMEM((tq, 2 * tq), f32), pltpu.VMEM((rows, LANES), f32),
                        pltpu.VMEM((rows, LANES), f32), pltpu.VMEM((rows, LANES), f32),
                        pltpu.VMEM((rows, LANES), f32)],
        compiler_params=_cparams(("arbitrary", "arbitrary"), VMEM_LIMIT),
        name="slc_prompt",
    )(q, kt, v1, sel, expand, bias)


def _slc_sample_kernel(pt_ref, pool_ref, q_ref, sel_ref, ex_ref, kn_ref, vn_ref, bl_ref, bn_ref, exn_ref, o_ref,
                       buf, sem, madd_ref, m_ref, l_ref, acc_ref, *, ch0, pgs, nchunk, ppt):
    b = pl.program_id(0)
    c = pl.program_id(1)
    step = b * nchunk + c
    nsteps = pl.num_programs(0) * nchunk
    slot = step % 2
    psz = buf.shape[-1]
    rows = q_ref.shape[0]
    ntile = pgs // ppt
    tk = ppt * psz
    last = c == nchunk - 1

    def page_copy(bb, cc, p, sl):
        page = pt_ref[bb, cc * pgs + p]
        return pltpu.make_async_copy(pool_ref.at[page, pl.ds(ch0, 2)], buf.at[sl, p], sem.at[sl])

    def start(st, sl):
        bb = st // nchunk
        cc = st % nchunk

        def body(p, carry):
            page_copy(bb, cc, p, sl).start()
            return carry
        lax.fori_loop(0, pgs, body, 0)

    @pl.when(step == 0)
    def _():
        start(step, slot)

    @pl.when(step + 1 < nsteps)
    def _():
        start(step + 1, 1 - slot)

    def wait_body(p, carry):
        page_copy(b, c, p, slot).wait()
        return carry
    lax.fori_loop(0, pgs, wait_body, 0)

    @pl.when(c == 0)
    def _():
        m_ref[...] = jnp.full(m_ref.shape, NEG, f32)
        l_ref[...] = jnp.zeros(l_ref.shape, f32)
        acc_ref[...] = jnp.zeros(acc_ref.shape, f32)

    madd_ref[...] = jnp.where(_dot(sel_ref[...], ex_ref[...]) > 0.5, 0.0, NEG)

    def tile(j, with_bias):
        kt = jnp.concatenate([buf[slot, j * ppt + p, 0] for p in range(ppt)], axis=1).astype(bf16)
        vt = jnp.concatenate([buf[slot, j * ppt + p, 1] for p in range(ppt)], axis=1).astype(bf16)
        col0 = pl.multiple_of(j * tk, tk)
        a = _dot(q_ref[...], kt) + madd_ref[:, pl.ds(col0, tk)]
        if with_bias:
            a = a + bl_ref[...]
        _softmax_chunk(a, lambda p: _dot_nt(p, vt), m_ref, l_ref, acc_ref, slice(None))

    def plain(j, carry):
        tile(j, False)
        return carry
    lax.fori_loop(0, ntile - jnp.where(last, 1, 0), plain, 0)

    @pl.when(last)
    def _():
        tile(ntile - 1, True)
        madd_new = jnp.where(_dot(sel_ref[...], exn_ref[...]) > 0.5, 0.0, NEG)
        vn = vn_ref[...]
        a = _dot(q_ref[...], kn_ref[...]) + madd_new + bn_ref[...]
        _softmax_chunk(a, lambda p: _dot_nt(p, vn), m_ref, l_ref, acc_ref, slice(None))
        full = acc_ref[...] / l_ref[...]
        r = rows // KV
        o_ref[...] = jnp.concatenate(
            [full[h * r:(h + 1) * r, h * HEAD_DIM:(h + 1) * HEAD_DIM] for h in range(KV)], axis=0)


def _slc_sample(pool, ch0, pt, qbd, selr, expand, knew, vnew, bias_last, bias_new, expand_new, ppt=4):
    db, npg = pt.shape
    psz = pool.shape[-1]
    rows = qbd.shape[1]
    nkd = KV * HEAD_DIM
    pgs = min(npg, 32)
    nchunk = npg // pgs
    ppt = min(ppt, pgs)
    nbp = selr.shape[-1]
    ck = pgs * psz
    kern = functools.partial(_slc_sample_kernel, ch0=ch0, pgs=pgs, nchunk=nchunk, ppt=ppt)
    return pl.pallas_call(
        kern,
        out_shape=jax.ShapeDtypeStruct((db, rows, HEAD_DIM), f32),
        grid_spec=pltpu.PrefetchScalarGridSpec(
            num_scalar_prefetch=1,
            grid=(db, nchunk),
            in_specs=[pl.BlockSpec(memory_space=pl.ANY),
                      pl.BlockSpec((None, rows, nkd), lambda b, c, pt: (b, 0, 0)),
                      pl.BlockSpec((None, rows, nbp), lambda b, c, pt: (b, 0, 0)),
                      pl.BlockSpec((nbp, ck), lambda b, c, pt: (0, c)),
                      pl.BlockSpec((None, nkd, BLK), lambda b, c, pt: (b, 0, 0)),
                      pl.BlockSpec((None, nkd, BLK), lambda b, c, pt: (b, 0, 0)),
                      pl.BlockSpec((rows, ppt * psz), lambda b, c, pt: (0, 0)),
                      pl.BlockSpec((rows, BLK), lambda b, c, pt: (0, 0)),
                      pl.BlockSpec((nbp, BLK), lambda b, c, pt: (0, 0))],
            out_specs=pl.BlockSpec((None, rows, HEAD_DIM), lambda b, c, pt: (b, 0, 0)),
            scratch_shapes=[pltpu.VMEM((2, pgs, 2, nkd, psz), f32),
                            pltpu.SemaphoreType.DMA((2,)),
                            pltpu.VMEM((rows, ck), f32),
                            pltpu.VMEM((rows, 1), f32), pltpu.VMEM((rows, 1), f32),
                            pltpu.VMEM((rows, nkd), f32)]),
        compiler_params=_cparams(("arbitrary", "arbitrary"), VMEM_LIMIT),
        name="slc_sample",
    )(pt, pool, qbd, selr, expand, knew, vnew, bias_last, bias_new, expand_new)


def _merge_kernel(gl_ref, oc_ref, os_ref, ow_ref, ex_ref, o_ref):
    gate = jax.nn.sigmoid(gl_ref[...])
    hi = gate.astype(bf16)
    lo = (gate - hi.astype(f32)).astype(bf16)
    acc = None
    for br, ref in enumerate((oc_ref, os_ref, ow_ref)):
        ex = ex_ref[br]
        gx = _dot(hi, ex) + _dot(lo, ex)
        term = gx * ref[...]
        acc = term if acc is None else acc + term
    o_ref[...] = acc.astype(bf16)


def _merge(glog, oc, osl, ow, expand, tm=TOKEN_TILE):
    t, n = oc.shape
    return pl.pallas_call(
        _merge_kernel,
        out_shape=jax.ShapeDtypeStruct((t, n), bf16),
        grid=(t // tm,),
        in_specs=[pl.BlockSpec((tm, LANES), lambda i: (i, 0)),
                  pl.BlockSpec((tm, n), lambda i: (i, 0)),
                  pl.BlockSpec((tm, n), lambda i: (i, 0)),
                  pl.BlockSpec((tm, n), lambda i: (i, 0)),
                  pl.BlockSpec((3, LANES, n), lambda i: (0, 0, 0))],
        out_specs=pl.BlockSpec((tm, n), lambda i: (i, 0)),
        compiler_params=_cparams(("arbitrary",)),
        name="nsa_merge",
    )(glog, oc, osl, ow, expand)


def _moe_kernel(te_ref, ns_ref, ic_ref, in_ref, h_ref, wg_ref, wl_ref, bg_ref, bl_ref, wd_ref, bd_ref, o_ref,
                xbuf, sem, xs, wg_s, wl_s, wd_s, *, sub):
    t = pl.program_id(0)
    f = pl.program_id(1)
    nt = pl.num_programs(0)
    nsub = ns_ref[t]

    half = xbuf.shape[1]
    unroll = 8

    def row_copy(idx_ref, r):
        return pltpu.make_async_copy(h_ref.at[pl.ds(idx_ref[0, r], 1)], xbuf.at[pl.ds(r, 1)], sem.at[0])

    def for_rows(nrows, fn):
        def body(i, carry):
            for u in range(unroll):
                fn(i * unroll + u)
            return carry
        lax.fori_loop(0, nrows // unroll, body, 0)

    def start(idx_ref, nrows):
        for_rows(nrows, lambda r: row_copy(idx_ref, r).start())

    @pl.when(f == 0)
    def _():
        o_ref[...] = jnp.zeros(o_ref.shape, f32)

        @pl.when(jnp.logical_and(t == 0, nsub > 0))
        def _():
            start(ic_ref, nsub * sub)

        @pl.when(nsub > 0)
        def _():
            for_rows(nsub * sub, lambda r: row_copy(ic_ref, r).wait())

            def unpack_body(s, carry):
                r = pl.multiple_of(s * sub, sub)
                w = xbuf[pl.ds(r, sub), :]
                lo = lax.bitcast_convert_type(lax.shift_left(w, jnp.uint32(16)), f32)
                hi = lax.bitcast_convert_type(w & jnp.uint32(0xFFFF0000), f32)
                xs[pl.ds(r, sub), :half] = lo.astype(bf16)
                xs[pl.ds(r, sub), half:] = hi.astype(bf16)
                return carry
            lax.fori_loop(0, nsub, unpack_body, 0)

        nnext = ns_ref[jnp.minimum(t + 1, nt - 1)]

        @pl.when(jnp.logical_and(t + 1 < nt, nnext > 0))
        def _():
            start(in_ref, nnext * sub)

    @pl.when(nsub > 0)
    def _():
        wg_s[...] = wg_ref[...].astype(bf16)
        wl_s[...] = wl_ref[...].astype(bf16)
        wd_s[...] = wd_ref[...].astype(bf16)
        first = jnp.where(f == 0, 1.0, 0.0)

        def chunk(r, rows):
            x = xs[pl.ds(r, rows), :]
            glu = jnp.minimum(_dot(x, wg_s[...]) + bg_ref[...], SWIGLU_LIMIT)
            lin = jnp.clip(_dot(x, wl_s[...]) + bl_ref[...], -SWIGLU_LIMIT, SWIGLU_LIMIT)
            act = glu * jax.nn.sigmoid(SWIGLU_ALPHA * glu) * (lin + 1.0)
            y = _dot(act.astype(bf16), wd_s[...])
            o_ref[pl.ds(r, rows), :] += y + first * bd_ref[...]

        def pair(j, carry):
            chunk(pl.multiple_of(j * 2 * sub, 2 * sub), 2 * sub)
            return carry
        lax.fori_loop(0, nsub // 2, pair, 0)

        @pl.when(nsub % 2 == 1)
        def _():
            chunk(pl.multiple_of((nsub - 1) * sub, sub), sub)


def _moe_experts(h, src_tok, tile_e, tile_ns, layer, w_up, b_up, w_down, b_down):
    t = h.shape[0]
    d = 2 * h.shape[1]
    nt, _, tm = src_tok.shape
    ff = w_down.shape[2]
    fc = min(MOE_FC, ff)
    nf = ff // fc

    def fidx(ti, f, ns):
        return jnp.where(ns[ti] > 0, f, 0)

    idx_c = pl.BlockSpec((None, 1, tm), lambda ti, f, te, ns: (ti, 0, 0), memory_space=pltpu.SMEM)
    idx_n = pl.BlockSpec((None, 1, tm), lambda ti, f, te, ns: (jnp.minimum(ti + 1, nt - 1), 0, 0),
                         memory_space=pltpu.SMEM)
    return pl.pallas_call(
        functools.partial(_moe_kernel, sub=MOE_SUB),
        out_shape=jax.ShapeDtypeStruct((nt * tm, d), f32),
        grid_spec=pltpu.PrefetchScalarGridSpec(
            num_scalar_prefetch=2,
            grid=(nt, nf),
            in_specs=[idx_c, idx_n,
                      pl.BlockSpec(memory_space=pl.ANY),
                      pl.BlockSpec((None, None, d, fc), lambda ti, f, te, ns: (layer, te[ti], 0, fidx(ti, f, ns))),
                      pl.BlockSpec((None, None, d, fc),
                                   lambda ti, f, te, ns: (layer, te[ti], 0, nf + fidx(ti, f, ns))),
                      pl.BlockSpec((None, None, 1, fc), lambda ti, f, te, ns: (layer, te[ti], 0, fidx(ti, f, ns))),
                      pl.BlockSpec((None, None, 1, fc),
                                   lambda ti, f, te, ns: (layer, te[ti], 0, nf + fidx(ti, f, ns))),
                      pl.BlockSpec((None, None, fc, d), lambda ti, f, te, ns: (layer, te[ti], fidx(ti, f, ns), 0)),
                      pl.BlockSpec((None, None, 1, d), lambda ti, f, te, ns: (layer, te[ti], 0, 0))],
            out_specs=pl.BlockSpec((tm, d), lambda ti, f, te, ns: (ti, 0)),
            scratch_shapes=[pltpu.VMEM((tm, d // 2), jnp.uint32), pltpu.SemaphoreType.DMA((1,)),
                            pltpu.VMEM((tm, d), bf16),
                            pltpu.VMEM((d, fc), bf16), pltpu.VMEM((d, fc), bf16), pltpu.VMEM((fc, d), bf16)]),
        compiler_params=_cparams(("arbitrary", "arbitrary"), VMEM_LIMIT),
        name="moe_experts",
    )(tile_e, tile_ns, src_tok, src_tok, h, w_up, w_up, b_up, b_up, w_down, b_down)


def _combine_kernel(ic_ref, in_ref, y_ref, x_ref, tw_ref, gt_ref, o_ref, buf, sem):
    step = pl.program_id(0)
    nsteps = pl.num_programs(0)
    slot = step % 2
    nrow = buf.shape[1]

    unroll = 8

    def row_copy(idx_ref, r, sl):
        return pltpu.make_async_copy(y_ref.at[pl.ds(idx_ref[0, r], 1)], buf.at[sl, pl.ds(r, 1)], sem.at[sl])

    def for_rows(fn):
        def body(i, carry):
            for u in range(unroll):
                fn(i * unroll + u)
            return carry
        lax.fori_loop(0, nrow // unroll, body, 0)

    @pl.when(step == 0)
    def _():
        for_rows(lambda r: row_copy(ic_ref, r, slot).start())

    @pl.when(step + 1 < nsteps)
    def _():
        for_rows(lambda r: row_copy(in_ref, r, 1 - slot).start())

    for_rows(lambda r: row_copy(ic_ref, r, slot).wait())

    tt, d = x_ref.shape
    rg = gt_ref.shape[0]
    tw = tw_ref[...]
    acc = None
    for k in range(TOP_K):
        term = tw[:, k:k + 1] * buf[slot, pl.ds(k * tt, tt), :]
        acc = term if acc is None else acc + term
    if tt >= rg:
        upd = (acc.reshape(tt // rg, rg, d) * gt_ref[...][None]).reshape(tt, d)
    else:
        part = step % (rg // tt)
        upd = acc * gt_ref[pl.ds(pl.multiple_of(part * tt, tt), tt), :]
    o_ref[...] = x_ref[...] + upd


def _moe_combine(y, pos, x, topw, mods, m_idx, tp):
    t, d = x.shape
    tt = MOE_TT
    nsteps = t // tt
    rg = mods.shape[1] // 2
    npt = tp // tt
    n_idx = TOP_K * tt
    idx_c = pl.BlockSpec((None, 1, n_idx), lambda i: (i, 0, 0), memory_space=pltpu.SMEM)
    idx_n = pl.BlockSpec((None, 1, n_idx), lambda i: (jnp.minimum(i + 1, nsteps - 1), 0, 0),
                         memory_space=pltpu.SMEM)
    return pl.pallas_call(
        _combine_kernel,
        out_shape=jax.ShapeDtypeStruct((t, d), f32),
        grid=(nsteps,),
        in_specs=[idx_c, idx_n,
                  pl.BlockSpec(memory_space=pl.ANY),
                  pl.BlockSpec((tt, d), lambda i: (i, 0)),
                  pl.BlockSpec((tt, LANES), lambda i: (i, 0)),
                  pl.BlockSpec((None, rg, d), lambda i: (m_idx, jnp.where(i < npt, 0, 1), 2))],
        out_specs=pl.BlockSpec((tt, d), lambda i: (i, 0)),
        scratch_shapes=[pltpu.VMEM((2, n_idx, d), f32), pltpu.SemaphoreType.DMA((2,))],
        compiler_params=_cparams(("arbitrary",)),
        name="moe_combine",
    )(pos, pos, y, x, topw, mods)


def _moe_layer(x, g, mods, m_idx, tp, layer, router_w, router_b, w_up, b_up, w_down, b_down):
    t, d = x.shape
    ne = router_w.shape[1]
    ff = w_down.shape[2]
    tm, sub, tt = MOE_TM, MOE_SUB, MOE_TT

    rw = jnp.pad(router_w, ((0, 0), (0, LANES - ne)))
    whi = rw.astype(bf16)
    wlo = (rw - whi.astype(f32)).astype(bf16)
    rb = jnp.pad(router_b.reshape(1, ne), ((0, 0), (0, LANES - ne)), constant_values=-jnp.inf)
    h, topi, topw = _modnorm(x, g, mods, m_idx, tp, router=(whi, wlo, rb))

    e_flat = topi[:, :TOP_K].reshape(-1)
    npair = t * TOP_K
    onehot = (e_flat[:, None] == jnp.arange(ne, dtype=i32)[None, :]).astype(i32)
    counts = jnp.sum(onehot, axis=0)
    order = jnp.argsort(e_flat, stable=True).astype(i32)
    inv = jnp.argsort(order).astype(i32)
    tiles_e = (counts + tm - 1) // tm
    cum_tiles = jnp.cumsum(tiles_e)
    tile_start = cum_tiles - tiles_e
    grp_start = jnp.cumsum(counts) - counts
    nt = npair // tm + ne
    tile_ids = jnp.arange(nt, dtype=i32)
    tile_e = jnp.clip(jnp.searchsorted(cum_tiles, tile_ids, side="right"), 0, ne - 1).astype(i32)
    tile_off = (tile_ids - tile_start[tile_e]) * tm
    rows_in_tile = jnp.where(tile_ids < cum_tiles[-1], jnp.clip(counts[tile_e] - tile_off, 0, tm), 0)
    tile_ns = ((rows_in_tile + sub - 1) // sub).astype(i32)
    within = jnp.arange(tm, dtype=i32)[None, :]
    sorted_idx = jnp.clip((grp_start[tile_e] + tile_off)[:, None] + within, 0, npair - 1)
    src_tok = jnp.where(within < rows_in_tile[:, None], order[sorted_idx] // TOP_K, 0)
    pair_tile_start = jnp.sum(onehot * tile_start[None, :], axis=1)
    pair_grp_start = jnp.sum(onehot * grp_start[None, :], axis=1)
    pos = pair_tile_start * tm + (inv - pair_grp_start)

    y = _moe_experts(h, src_tok.reshape(nt, 1, tm).astype(i32), tile_e, tile_ns, layer,
                     w_up, b_up.reshape(b_up.shape[0], ne, 1, 2 * ff), w_down,
                     b_down.reshape(b_down.shape[0], ne, 1, d))
    pos_steps = pos.astype(i32).reshape(t // tt, tt, TOP_K).transpose(0, 2, 1).reshape(t // tt, 1, TOP_K * tt)
    return _moe_combine(y, pos_steps, x, topw, mods, m_idx, tp)


def _rms_kernel(x_ref, g_ref, o_ref):
    x = x_ref[...]
    o_ref[...] = x * lax.rsqrt(jnp.mean(x * x, axis=-1, keepdims=True) + EPS) * g_ref[...]


def _rmsnorm(x, g, tm=TOKEN_TILE):
    t, d = x.shape
    return pl.pallas_call(
        _rms_kernel,
        out_shape=jax.ShapeDtypeStruct((t, d), f32),
        grid=(t // tm,),
        in_specs=[pl.BlockSpec((tm, d), lambda i: (i, 0)), pl.BlockSpec((1, d), lambda i: (0, 0))],
        out_specs=pl.BlockSpec((tm, d), lambda i: (i, 0)),
        compiler_params=_cparams(("arbitrary",)),
        name="final_norm",
    )(x, g)


def _heads_prompt(a, tp):
    return a[:tp].reshape(tp, KV, G, HEAD_DIM).transpose(1, 2, 0, 3)


def _heads_sample(a, tp, ts, db):
    return a[tp:].reshape(ts, db, KV, G, HEAD_DIM).transpose(1, 2, 3, 0, 4).reshape(db, KV, G * ts, HEAD_DIM)


def _unheads(op, osm, tp, ts, db):
    a = op.transpose(2, 0, 1, 3).reshape(tp, N_HEADS * HEAD_DIM)
    b = osm.reshape(db, KV, G, ts, HEAD_DIM).transpose(3, 0, 1, 2, 4).reshape(ts * db, N_HEADS * HEAD_DIM)
    return jnp.concatenate([a, b], axis=0)


def _kv_prompt(a, tp):
    return a[:tp].reshape(tp, KV, HEAD_DIM).transpose(1, 0, 2)


def _kv_sample(a, tp, ts, db):
    return a[tp:].reshape(ts, db, KV, HEAD_DIM).transpose(1, 0, 2, 3)


def _pages_dim_major(rows, psz):
    n = rows.shape[0] // psz
    return rows.reshape(n, psz, -1, KV, HEAD_DIM).transpose(0, 2, 3, 4, 1)


def _block_expand(nb, nkeys, first_block=0):
    blk = first_block + jnp.arange(nkeys, dtype=i32) // BLK
    return (jnp.arange(nb, dtype=i32)[:, None] == blk[None, :]).astype(bf16)


def kernel(x_prompt, x_sample, state_swa, cache_nsa, state_nsa_win, page_table, c_prompt, c_sample,
           rel_table, w_qkv_a, sink_a, w_o_a, w_q_b, w_gate_b, w_o_b, w_kv_shared, cmp_pe, cmp_w1, cmp_w2,
           router_w, router_b, w_up, b_up, w_down, b_down, norm_g, ada_w, ada_b,
           kv_norm_g, ada_kv_w, ada_kv_b, final_norm_g):
    bsz, tp, d = x_prompt.shape
    db, ts, _ = x_sample.shape
    n_a = w_qkv_a.shape[0]
    depth = norm_g.shape[0]
    npg = page_table.shape[1]
    psz = cache_nsa.shape[1]
    past = npg * psz
    wb = state_swa.shape[2]
    nq = N_HEADS * HEAD_DIM
    nk = KV * HEAD_DIM
    scale = HEAD_DIM ** -0.5
    assert bsz == 1 and wb == WINDOW and state_nsa_win.shape[1] == WINDOW
    assert psz == 2 * BLK and past % BLK == 0 and ts <= BLK and tp % SLC_TQ == 0 and tp % TOKEN_TILE == 0
    t = tp + ts * db

    x = jnp.concatenate([x_prompt[0], x_sample.transpose(1, 0, 2).reshape(ts * db, d)], axis=0)
    cmat = jnp.concatenate([jnp.broadcast_to(c_prompt, (db, d)), c_sample], axis=0)
    mods = _ada(cmat, ada_w.reshape(depth * 2, d, 3 * d), ada_b.reshape(depth * 2, 1, 3 * d))
    mods_kv = _ada(cmat, ada_kv_w[None], ada_kv_b[None, None])

    lut = rel_table[_rel_bucket(jnp.arange(LUT_N))].astype(f32)
    c31 = lut[LUT_N - 1]
    bias_band = _toeplitz_bias(lut, WINDOW, 2 * WINDOW, WINDOW, window=WINDOW)
    bias_step = _toeplitz_bias(lut, ts, wb + ts, wb, window=WINDOW)

    swa_p, swa_s = [], []
    for layer in range(depth):
        if layer == n_a:
            h_kv = _modnorm(x, kv_norm_g.reshape(1, d), mods_kv, 0, tp)
            kvr = _mm(h_kv, w_kv_shared.astype(bf16))
            kv_s = kvr[tp:].reshape(ts, db, 6 * nk).transpose(1, 0, 2)
            nsa_rows_p = kvr[:tp, :4 * nk].reshape(1, tp, 4, KV, HEAD_DIM)
            nsa_rows_s = kv_s[:, :, :4 * nk].reshape(db, ts, 4, KV, HEAD_DIM)
            nsa_win_p = kvr[tp - WINDOW:tp, 4 * nk:].reshape(1, WINDOW, 2, KV, HEAD_DIM)
            nsa_win_s = jnp.concatenate(
                [state_nsa_win, kv_s[:, :, 4 * nk:].reshape(db, ts, 2, KV, HEAD_DIM)], axis=1)[:, -WINDOW:]

            nhid = cmp_w1.shape[-1]
            pe_t = jnp.tile(cmp_pe.transpose(0, 2, 1), (1, 1, 2)).reshape(2, HEAD_DIM, 1, psz)
            w1t = cmp_w1.reshape(2, BLK, HEAD_DIM // 2, 2, nhid).transpose(0, 2, 3, 1, 4)
            eye2 = jnp.eye(2, dtype=f32)
            w1p = (w1t[:, :, :, None, :, None, :] * eye2[None, None, None, :, None, :, None]).reshape(
                2, HEAD_DIM // 2, 2 * psz, 2 * nhid).astype(bf16)
            w2b = cmp_w2.astype(bf16)
            npg_p = tp // psz
            cmp_p = _compress(_pages_dim_major(kvr[:tp, :2 * nk], psz), 0,
                              jnp.arange(npg_p, dtype=i32)[None], pe_t, w1p, w2b)[0]
            pool = cache_nsa.transpose(0, 2, 3, 4, 1)
            cmp_s = _compress(pool, 0, page_table, pe_t, w1p, w2b)
            tail = jnp.pad(kv_s[:, :, :2 * nk], ((0, 0), (0, psz - ts), (0, 0))).reshape(db * psz, 2 * nk)
            cmp_t = _compress(_pages_dim_major(tail, psz), 0, jnp.arange(db, dtype=i32)[None], pe_t, w1p, w2b)[0]
            cmp_t = cmp_t.reshape(2, KV, db, 2, HEAD_DIM)[:, :, :, 0].transpose(2, 0, 1, 3)
            nb_s = past // BLK + 1
            nbp = ((nb_s + 255) // 256) * 256
            cmp_s = jnp.concatenate(
                [cmp_s, cmp_t[:, :, :, None], jnp.zeros((db, 2, KV, nbp - nb_s, HEAD_DIM), f32)], axis=3)
            kc_p, vc_p = cmp_p[0].astype(bf16), cmp_p[1].astype(bf16)
            kc_s, vc_s = cmp_s[:, 0].astype(bf16), cmp_s[:, 1].astype(bf16)

        g_mix = norm_g[layer, 0].reshape(1, d)
        h = _modnorm(x, g_mix, mods, 2 * layer, tp)
        if layer < n_a:
            y = _mm(h, w_qkv_a[layer].astype(bf16))
            q, k, v = y[:, :nq], y[:, nq:nq + nk], y[:, nq + nk:]
            sink = sink_a[layer].astype(f32)
            o_p = _band_attn(_heads_prompt(q, tp).astype(bf16), _kv_prompt(k, tp).astype(bf16),
                             _kv_prompt(v, tp).astype(bf16), bias_band, sink)
            k_s, v_s = _kv_sample(k, tp, ts, db), _kv_sample(v, tp, ts, db)
            kk = jnp.concatenate([state_swa[layer, :, :, 0], k_s], axis=1).transpose(0, 2, 1, 3)
            vv = jnp.concatenate([state_swa[layer, :, :, 1], v_s], axis=1).transpose(0, 2, 1, 3)
            o_s = _step_attn(_heads_sample(q, tp, ts, db).astype(bf16), kk.astype(bf16), vv.astype(bf16),
                             bias_step, _head_rows(sink, ts))
            swa_p.append(jnp.stack([k[tp - wb:tp].reshape(1, wb, KV, HEAD_DIM),
                                    v[tp - wb:tp].reshape(1, wb, KV, HEAD_DIM)], axis=2))
            swa_s.append(jnp.concatenate([state_swa[layer], jnp.stack([k_s, v_s], axis=2)], axis=1)[:, -wb:])
            x = _mm_res(_unheads(o_p, o_s, tp, ts, db), w_o_a[layer].astype(bf16), x, mods, 2 * layer, tp)
        else:
            j = layer - n_a
            qf = _mm(h, w_q_b[j].astype(bf16))
            ngate = 3 * N_HEADS
            glog = _mm(h, jnp.pad(w_gate_b[j], ((0, 0), (0, LANES - ngate))).astype(bf16))
            q_p = _heads_prompt(qf, tp).astype(bf16)
            q_s = _heads_sample(qf, tp, ts, db).astype(bf16)

            tq = WINDOW
            iw = jnp.arange(tq)
            jj = jnp.arange(tq // BLK + 2)
            dist_c = iw[:, None] - (BLK * (jj[None, :] - 2) + BLK - 1)
            bc = _dist_bias(lut, dist_c, dist_c >= 0)
            bc = bc.transpose(0, 2, 1)[:, :, None, :]
            nqt = tp // tq
            qt_p = _heads_prompt(qf, tp).reshape(KV, G, nqt, tq, HEAD_DIM).transpose(0, 2, 4, 1, 3)
            qt_p = qt_p.reshape(KV, nqt, HEAD_DIM, G * tq).astype(bf16)
            oc_t, sel_t = _cmp_prompt(qt_p, kc_p, vc_p.transpose(0, 2, 1), bc,
                                      _head_rows(c31, tq).transpose(0, 2, 1), G, tq)
            oc_p = oc_t.reshape(KV, nqt, HEAD_DIM, G, tq).transpose(0, 3, 1, 4, 2).reshape(KV, G, tp, HEAD_DIM)
            sel_p = sel_t.transpose(0, 2, 1)
            dist_cs = (past + jnp.arange(ts))[:, None] - (BLK * jnp.arange(nbp)[None, :] + BLK - 1)
            bias_cs = _dist_bias(lut, dist_cs, dist_cs >= 0)
            cur_s = ((past + jnp.arange(ts)) // BLK).astype(i32).reshape(ts, 1)
            oc_s, sel_s = _cmp_sample(q_s, kc_s, vc_s, bias_cs, cur_s)

            tqs = SLC_TQ
            bias_sl = _toeplitz_bias(lut, tqs, 2 * tqs, tqs, sub=c31)
            kst_p = _kv_prompt(kvr[:, 2 * nk:3 * nk], tp).transpose(0, 2, 1).astype(bf16)
            vs_p = _kv_prompt(kvr[:, 3 * nk:4 * nk], tp)
            vs1_p = jnp.concatenate([vs_p, jnp.ones((KV, tp, 1), f32),
                                     jnp.zeros((KV, tp, LANES - HEAD_DIM - 1), f32)], axis=2).astype(bf16)
            os_p = _slc_prompt((_heads_prompt(qf, tp) * scale).astype(bf16), kst_p, vs1_p, sel_p,
                               _block_expand(sel_p.shape[-1], tp), bias_sl, tqs)

            rows = KV * G * ts
            eye = jnp.eye(KV, dtype=f32)
            qs_scaled = _heads_sample(qf, tp, ts, db) * scale
            qbd = (qs_scaled[:, :, :, None, :] * eye[None, :, None, :, None]).reshape(db, rows, nk).astype(bf16)
            selr = jnp.broadcast_to(sel_s[:, :, None], (db, KV, G, ts, nbp)).reshape(db, rows, nbp)
            ppt = min(SLC_PAGES_PER_TILE, npg)
            tk = ppt * psz
            bias_l = _toeplitz_bias(lut, ts, tk, tk, sub=c31).reshape(rows, tk)
            bias_n = _toeplitz_bias(lut, ts, BLK, 0, sub=c31).reshape(rows, BLK)
            new_t = jnp.pad(kv_s[:, :, 2 * nk:4 * nk], ((0, 0), (0, BLK - ts), (0, 0))).transpose(0, 2, 1)
            knew, vnew = new_t[:, :nk].astype(bf16), new_t[:, nk:].astype(bf16)
            os_s = _slc_sample(pool.reshape(pool.shape[0], 4, nk, psz), 2, page_table, qbd, selr,
                               _block_expand(nbp, past), knew, vnew, bias_l, bias_n,
                               _block_expand(nbp, BLK, past // BLK), ppt)
            os_s = os_s.reshape(db, KV, G * ts, HEAD_DIM)

            kw_p = _kv_prompt(kvr[:, 4 * nk:5 * nk], tp).astype(bf16)
            vw_p = _kv_prompt(kvr[:, 5 * nk:6 * nk], tp).astype(bf16)
            ow_p = _band_attn(q_p, kw_p, vw_p, bias_band)
            kw_s = kv_s[:, :, 4 * nk:5 * nk].reshape(db, ts, KV, HEAD_DIM)
            vw_s = kv_s[:, :, 5 * nk:6 * nk].reshape(db, ts, KV, HEAD_DIM)
            kk = jnp.concatenate([state_nsa_win[:, :, 0], kw_s], axis=1).transpose(0, 2, 1, 3)
            vv = jnp.concatenate([state_nsa_win[:, :, 1], vw_s], axis=1).transpose(0, 2, 1, 3)
            ow_s = _step_attn(q_s, kk.astype(bf16), vv.astype(bf16), bias_step)

            head_of_lane = jnp.arange(nq) // HEAD_DIM
            expand = (jnp.arange(LANES)[None, :, None]
                      == (jnp.arange(3)[:, None, None] * N_HEADS + head_of_lane[None, None, :])).astype(bf16)
            merged = _merge(glog, _unheads(oc_p, oc_s, tp, ts, db), _unheads(os_p, os_s, tp, ts, db),
                            _unheads(ow_p, ow_s, tp, ts, db), expand)
            x = _mm_res(merged, w_o_b[j].astype(bf16), x, mods, 2 * layer, tp)

        x = _moe_layer(x, norm_g[layer, 1].reshape(1, d), mods, 2 * layer + 1, tp, layer,
                       router_w[layer], router_b[layer], w_up, b_up, w_down, b_down)

    y = _rmsnorm(x, final_norm_g.reshape(1, d))
    y_prompt = y[:tp].reshape(1, tp, d)
    y_sample = y[tp:].reshape(ts, db, d).transpose(1, 0, 2)
    return (y_prompt, y_sample, jnp.stack(swa_p), jnp.stack(swa_s), nsa_rows_p, nsa_rows_s, nsa_win_p, nsa_win_s)
```

```python
import functools
import math

import jax
import jax.numpy as jnp
from jax import lax
from jax.experimental import pallas as pl
from jax.experimental.pallas import tpu as pltpu

f32 = jnp.float32
bf16 = jnp.bfloat16
i32 = jnp.int32

N_HEADS = 32
HEAD_DIM = 64
KV = 4
G = N_HEADS // KV
WINDOW = 128
BLK = 64
N_SEL = 16
TOP_K = 4
SWIGLU_LIMIT = 7.0
SWIGLU_ALPHA = 1.702
REL_BUCKETS = 32
REL_MAX_DIST = 128
EPS = 1e-6
NEG = -1e30
TINY = 1e-20
FORCE_SCORE = 1e4
LUT_N = 256
LANES = 128
VMEM_LIMIT = 56 * 1024 * 1024

TOKEN_TILE = 256
SLC_TQ = 256
SLC_ROWS = 128
SLC_PAGES_PER_TILE = 16
MOE_TM = 1280
MOE_SUB = 128
MOE_FC = 256
MOE_TT = 64


def _cparams(sem, vmem=None):
    return pltpu.CompilerParams(dimension_semantics=sem, vmem_limit_bytes=vmem)


def _col_tile(n, cap=512):
    tn = cap
    while n % tn:
        tn //= 2
    assert tn % LANES == 0
    return tn


def _dot(a, b):
    return jnp.dot(a, b, preferred_element_type=f32)


def _dot_nt(a, b):
    return lax.dot_general(a, b, (((1,), (1,)), ((), ())), preferred_element_type=f32)


def _rel_bucket(dist):
    n = jnp.maximum(dist, 0)
    max_exact = REL_BUCKETS // 2
    nf = jnp.maximum(n, 1).astype(f32)
    large = max_exact + (jnp.log(nf / max_exact) / math.log(REL_MAX_DIST / max_exact)
                         * (REL_BUCKETS - max_exact)).astype(i32)
    return jnp.where(n < max_exact, n, jnp.minimum(large, REL_BUCKETS - 1))


def _dist_bias(lut, dist, ok):
    b = lut[jnp.clip(dist, 0, LUT_N - 1)]
    b = jnp.where(ok[..., None], b, NEG)
    r, c = dist.shape
    return b.reshape(r, c, KV, G).transpose(2, 3, 0, 1).reshape(KV, G * r, c)


def _toeplitz_bias(lut, nrows, ncols, c, window=None, sub=None):
    lw = nrows + ncols
    dist = c + nrows - 1 - jnp.arange(lw)
    ok = dist >= 0
    if window is not None:
        ok = ok & (dist < window)
    vals = lut[jnp.clip(dist, 0, LUT_N - 1)]
    if sub is not None:
        vals = vals - sub[None, :]
    w = jnp.where(ok[:, None], vals, NEG).T
    skew = jnp.tile(w, (1, nrows))[:, :nrows * (lw - 1)].reshape(-1, nrows, lw - 1)
    return skew[:, :, nrows - 1:nrows - 1 + ncols].reshape(KV, G * nrows, ncols)


def _head_rows(vec, r):
    return jnp.broadcast_to(vec.reshape(KV, G, 1, 1), (KV, G, r, 1)).reshape(KV, G * r, 1).astype(f32)


def _ada_kernel(c_ref, w_ref, b_ref, o_ref):
    a = jax.nn.silu(c_ref[...]).astype(bf16)
    o_ref[...] = _dot(a, w_ref[...].astype(bf16)) + b_ref[...]


def _ada(cmat, w, b):
    nm, d, n = w.shape
    r = cmat.shape[0]
    tn = _col_tile(n)
    return pl.pallas_call(
        _ada_kernel,
        out_shape=jax.ShapeDtypeStruct((nm, r, n), f32),
        grid=(nm, n // tn),
        in_specs=[pl.BlockSpec((r, d), lambda m, j: (0, 0)),
                  pl.BlockSpec((None, d, tn), lambda m, j: (m, 0, j)),
                  pl.BlockSpec((None, 1, tn), lambda m, j: (m, 0, j))],
        out_specs=pl.BlockSpec((None, r, tn), lambda m, j: (m, 0, j)),
        compiler_params=_cparams(("arbitrary", "arbitrary")),
        name="ada_mods",
    )(cmat, w, b)


def _modulated(x_ref, g_ref, sh_ref, sc_ref):
    x = x_ref[...]
    tm, d = x.shape
    rg = sh_ref.shape[0]
    y = x * lax.rsqrt(jnp.mean(x * x, axis=-1, keepdims=True) + EPS) * g_ref[...]
    y = y.reshape(tm // rg, rg, d) * (1.0 + sc_ref[...])[None] + sh_ref[...][None]
    return y.reshape(tm, d)


def _modnorm_kernel(x_ref, g_ref, sh_ref, sc_ref, h_ref):
    h_ref[...] = _modulated(x_ref, g_ref, sh_ref, sc_ref).astype(bf16)


def _router_kernel(x_ref, g_ref, sh_ref, sc_ref, whi_ref, wlo_ref, rb_ref, h_ref, ti_ref, tw_ref):
    h = _modulated(x_ref, g_ref, sh_ref, sc_ref)
    hi = h.astype(bf16)
    lo = (h - hi.astype(f32)).astype(bf16)
    half = h.shape[1] // 2
    w_lo = lax.bitcast_convert_type(hi[:, :half].astype(f32), jnp.uint32)
    w_hi = lax.bitcast_convert_type(hi[:, half:].astype(f32), jnp.uint32)
    h_ref[...] = w_hi | lax.shift_right_logical(w_lo, jnp.uint32(16))
    logits = _dot(hi, whi_ref[...]) + (_dot(lo, whi_ref[...]) + _dot(hi, wlo_ref[...])) + rb_ref[...]
    lane = lax.broadcasted_iota(i32, logits.shape, 1).astype(f32)
    ti = jnp.zeros(logits.shape, f32)
    tv = jnp.zeros(logits.shape, f32)
    v0 = None
    den = None
    for k in range(TOP_K):
        m = jnp.max(logits, axis=-1, keepdims=True)
        idx = jnp.min(jnp.where(logits == m, lane, float(LANES)), axis=-1, keepdims=True)
        if k == 0:
            v0 = m
            e = jnp.ones_like(m)
            den = e
        else:
            e = jnp.exp(m - v0)
            den = den + e
        ti = jnp.where(lane == k, idx, ti)
        tv = jnp.where(lane == k, e, tv)
        logits = jnp.where(lane == idx, -jnp.inf, logits)
    ti_ref[...] = ti.astype(i32)
    tw_ref[...] = tv / den


def _mod_specs(d, rg, n_prompt_tiles, m_idx, cols):
    def spec(col):
        return pl.BlockSpec((None, rg, d), lambda i: (m_idx, jnp.where(i < n_prompt_tiles, 0, 1), col))
    return [spec(c) for c in cols]


def _modnorm(x, g, mods, m_idx, tp, router=None, tm=TOKEN_TILE):
    t, d = x.shape
    rg = mods.shape[1] // 2
    grid = (t // tm,)
    in_specs = [pl.BlockSpec((tm, d), lambda i: (i, 0)),
                pl.BlockSpec((1, d), lambda i: (0, 0))] + _mod_specs(d, rg, tp // tm, m_idx, (0, 1))
    h_spec = pl.BlockSpec((tm, d), lambda i: (i, 0))
    if router is None:
        return pl.pallas_call(
            _modnorm_kernel, out_shape=jax.ShapeDtypeStruct((t, d), bf16), grid=grid, in_specs=in_specs,
            out_specs=h_spec, compiler_params=_cparams(("arbitrary",)), name="modnorm",
        )(x, g, mods, mods)
    whi, wlo, rb = router
    in_specs += [pl.BlockSpec((d, LANES), lambda i: (0, 0)),
                 pl.BlockSpec((d, LANES), lambda i: (0, 0)),
                 pl.BlockSpec((1, LANES), lambda i: (0, 0))]
    lane_spec = pl.BlockSpec((tm, LANES), lambda i: (i, 0))
    return pl.pallas_call(
        _router_kernel,
        out_shape=(jax.ShapeDtypeStruct((t, d // 2), jnp.uint32), jax.ShapeDtypeStruct((t, LANES), i32),
                   jax.ShapeDtypeStruct((t, LANES), f32)),
        grid=grid, in_specs=in_specs,
        out_specs=(pl.BlockSpec((tm, d // 2), lambda i: (i, 0)), lane_spec, lane_spec),
        compiler_params=_cparams(("arbitrary",)), name="modnorm_router",
    )(x, g, mods, mods, whi, wlo, rb)


def _mm_kernel(a_ref, w_ref, o_ref):
    o_ref[...] = _dot(a_ref[...].astype(bf16), w_ref[...])


def _mm_res_kernel(a_ref, w_ref, x_ref, gt_ref, o_ref):
    acc = _dot(a_ref[...].astype(bf16), w_ref[...])
    tm, tn = acc.shape
    rg = gt_ref.shape[0]
    upd = acc.reshape(tm // rg, rg, tn) * gt_ref[...][None]
    o_ref[...] = x_ref[...] + upd.reshape(tm, tn)


def _mm(a, w, tm=TOKEN_TILE, tn=512):
    t, k = a.shape
    n = w.shape[1]
    tn = _col_tile(n, tn)
    return pl.pallas_call(
        _mm_kernel,
        out_shape=jax.ShapeDtypeStruct((t, n), f32),
        grid=(n // tn, t // tm),
        in_specs=[pl.BlockSpec((tm, k), lambda j, i: (i, 0)),
                  pl.BlockSpec((k, tn), lambda j, i: (0, j))],
        out_specs=pl.BlockSpec((tm, tn), lambda j, i: (i, j)),
        compiler_params=_cparams(("arbitrary", "arbitrary")),
        name="matmul",
    )(a, w)


def _mm_res(a, w, x, mods, m_idx, tp, tm=TOKEN_TILE, tn=512):
    t, k = a.shape
    n = w.shape[1]
    tn = _col_tile(n, tn)
    rg = mods.shape[1] // 2
    npt = tp // tm
    nj = n // tn
    return pl.pallas_call(
        _mm_res_kernel,
        out_shape=jax.ShapeDtypeStruct((t, n), f32),
        grid=(nj, t // tm),
        in_specs=[pl.BlockSpec((tm, k), lambda j, i: (i, 0)),
                  pl.BlockSpec((k, tn), lambda j, i: (0, j)),
                  pl.BlockSpec((tm, tn), lambda j, i: (i, j)),
                  pl.BlockSpec((None, rg, tn), lambda j, i: (m_idx, jnp.where(i < npt, 0, 1), 2 * nj + j))],
        out_specs=pl.BlockSpec((tm, tn), lambda j, i: (i, j)),
        compiler_params=_cparams(("arbitrary", "arbitrary")),
        name="matmul_residual",
    )(a, w, x, mods)


def _band_kernel(q_ref, kp_ref, kc_ref, vp_ref, vc_ref, b_ref, o_ref):
    g, tq, hd = q_ref.shape
    sink_keys = b_ref.shape[1] - 2 * tq
    has_sink = sink_keys > 0
    q = q_ref[...].reshape(g * tq, hd)
    ks = [kp_ref[...], kc_ref[...]]
    vs = [vp_ref[...], vc_ref[...]]
    if has_sink:
        ks.append(jnp.zeros((sink_keys, hd), bf16))
        vs.append(jnp.zeros((sink_keys, hd), bf16))
    kk = jnp.concatenate(ks, axis=0)
    vv = jnp.concatenate(vs, axis=0)
    s = _dot_nt(q, kk) * (hd ** -0.5) + b_ref[...]
    col = lax.broadcasted_iota(i32, s.shape, 1)
    first = pl.program_id(1) == 0
    s = jnp.where(jnp.logical_and(first, col < tq), NEG, s)
    m = jnp.max(s, axis=-1, keepdims=True)
    e = jnp.exp(s - m)
    den = jnp.sum(e, axis=-1, keepdims=True)
    o = _dot(e.astype(bf16), vv) / den
    o_ref[...] = o.reshape(g, tq, hd)


def _band_attn(q, k, v, bias, sink=None):
    _, g, t, hd = q.shape
    w = WINDOW
    if sink is not None:
        sink_cols = jnp.concatenate([_head_rows(sink, w), jnp.full((KV, g * w, LANES - 1), NEG, f32)], axis=2)
        bias = jnp.concatenate([bias, sink_cols], axis=2)
    nkeys = bias.shape[2]
    kspec_p = pl.BlockSpec((None, w, hd), lambda h, i: (h, jnp.maximum(i - 1, 0), 0))
    kspec_c = pl.BlockSpec((None, w, hd), lambda h, i: (h, i, 0))
    return pl.pallas_call(
        _band_kernel,
        out_shape=jax.ShapeDtypeStruct((KV, g, t, hd), f32),
        grid=(KV, t // w),
        in_specs=[pl.BlockSpec((None, g, w, hd), lambda h, i: (h, 0, i, 0)),
                  kspec_p, kspec_c, kspec_p, kspec_c,
                  pl.BlockSpec((None, g * w, nkeys), lambda h, i: (h, 0, 0))],
        out_specs=pl.BlockSpec((None, g, w, hd), lambda h, i: (h, 0, i, 0)),
        compiler_params=_cparams(("arbitrary", "arbitrary")),
        name="band_attn",
    )(q, k, k, v, v, bias)


def _step_kernel(q_ref, k_ref, v_ref, b_ref, *rest, has_sink):
    if has_sink:
        s_ref, o_ref = rest
    else:
        (o_ref,) = rest
    hd = q_ref.shape[-1]
    for h in range(KV):
        s = _dot_nt(q_ref[h], k_ref[h]) * (hd ** -0.5) + b_ref[h]
        m = jnp.max(s, axis=-1, keepdims=True)
        if has_sink:
            m = jnp.maximum(m, s_ref[h])
        e = jnp.exp(s - m)
        den = jnp.sum(e, axis=-1, keepdims=True)
        if has_sink:
            den = den + jnp.exp(s_ref[h] - m)
        o_ref[h] = _dot(e.astype(bf16), v_ref[h]) / den


def _step_attn(q, k, v, bias, sink=None):
    db, _, r, hd = q.shape
    l = k.shape[2]
    in_specs = [pl.BlockSpec((None, KV, r, hd), lambda b: (b, 0, 0, 0)),
                pl.BlockSpec((None, KV, l, hd), lambda b: (b, 0, 0, 0)),
                pl.BlockSpec((None, KV, l, hd), lambda b: (b, 0, 0, 0)),
                pl.BlockSpec((KV, r, l), lambda b: (0, 0, 0))]
    args = [q, k, v, bias]
    if sink is not None:
        in_specs.append(pl.BlockSpec((KV, r, 1), lambda b: (0, 0, 0)))
        args.append(sink)
    return pl.pallas_call(
        functools.partial(_step_kernel, has_sink=sink is not None),
        out_shape=jax.ShapeDtypeStruct((db, KV, r, hd), f32),
        grid=(db,),
        in_specs=in_specs,
        out_specs=pl.BlockSpec((None, KV, r, hd), lambda b: (b, 0, 0, 0)),
        compiler_params=_cparams(("arbitrary",)),
        name="step_attn",
    )(*args)


def _compress_kernel(pt_ref, pool_ref, pe_ref, w1_hbm, w2_ref, o_ref, buf, sem, w1_s, wsem, acc,
                     *, ch0, pgs, nchunk):
    b = pl.program_id(0)
    c = pl.program_id(1)
    step = b * nchunk + c
    nsteps = pl.num_programs(0) * nchunk
    slot = step % 2
    hd = buf.shape[3]
    nhid = w2_ref.shape[1]

    def page_copy(bb, cc, p, sl):
        page = pt_ref[bb, cc * pgs + p]
        return pltpu.make_async_copy(pool_ref.at[page, pl.ds(ch0, 2)], buf.at[sl, :, :, :, p, :], sem.at[sl])

    def start(st, sl):
        bb = st // nchunk
        cc = st % nchunk

        def body(p, carry):
            page_copy(bb, cc, p, sl).start()
            return carry
        lax.fori_loop(0, pgs, body, 0)

    @pl.when(step == 0)
    def _():
        w1_copy = pltpu.make_async_copy(w1_hbm, w1_s, wsem.at[0])
        w1_copy.start()
        start(step, slot)
        w1_copy.wait()

    @pl.when(step + 1 < nsteps)
    def _():
        start(step + 1, 1 - slot)

    def wait_body(p, carry):
        page_copy(b, c, p, slot).wait()
        return carry
    lax.fori_loop(0, pgs, wait_body, 0)

    for ch in range(2):
        for dp in range(hd // 2):
            parts = []
            for h in range(KV):
                pair = [(buf[slot, ch, h, 2 * dp + dd] + pe_ref[ch, 2 * dp + dd]).astype(bf16) for dd in range(2)]
                parts.append(jnp.concatenate(pair, axis=1))
            part = _dot(jnp.concatenate(parts, axis=0), w1_s[ch, dp])
            if dp == 0:
                acc[...] = part
            else:
                acc[...] += part
        hid = jax.nn.gelu(acc[...])
        for n in range(2):
            out = _dot(hid[:, n * nhid:(n + 1) * nhid].astype(bf16), w2_ref[ch])
            o_ref[ch, n] = out.reshape(KV, pgs, hd)


def _compress(pool, ch0, pt, pe_t, w1p, w2):
    nseq, npg = pt.shape
    psz = pool.shape[-1]
    pgs = min(npg, 64)
    nchunk = npg // pgs
    nhid = w2.shape[1]
    kern = functools.partial(_compress_kernel, ch0=ch0, pgs=pgs, nchunk=nchunk)
    out = pl.pallas_call(
        kern,
        out_shape=jax.ShapeDtypeStruct((nseq, 2, 2, KV, npg, HEAD_DIM), f32),
        grid_spec=pltpu.PrefetchScalarGridSpec(
            num_scalar_prefetch=1,
            grid=(nseq, nchunk),
            in_specs=[pl.BlockSpec(memory_space=pl.ANY),
                      pl.BlockSpec(pe_t.shape, lambda b, c, pt: (0, 0, 0, 0)),
                      pl.BlockSpec(memory_space=pl.ANY),
                      pl.BlockSpec(w2.shape, lambda b, c, pt: (0, 0, 0))],
            out_specs=pl.BlockSpec((None, 2, 2, KV, pgs, HEAD_DIM), lambda b, c, pt: (b, 0, 0, 0, c, 0)),
            scratch_shapes=[pltpu.VMEM((2, 2, KV, HEAD_DIM, pgs, psz), f32),
                            pltpu.SemaphoreType.DMA((2,)),
                            pltpu.VMEM(w1p.shape, bf16),
                            pltpu.SemaphoreType.DMA((1,)),
                            pltpu.VMEM((KV * pgs, 2 * nhid), f32)]),
        compiler_params=_cparams(("arbitrary", "arbitrary"), VMEM_LIMIT),
        name="compress",
    )(pt, pool, pe_t, w1p, w2)
    return out.transpose(0, 1, 3, 4, 2, 5).reshape(nseq, 2, KV, 2 * npg, HEAD_DIM)


def _cmp_core(q, kc, vc, bias, cur, g, tq):
    hd = q.shape[-1]
    nb = kc.shape[0]
    vis = bias > 0.5 * NEG
    s = _dot_nt(q, kc) * (hd ** -0.5) + bias
    m = jnp.max(s, axis=-1, keepdims=True)
    e = jnp.where(vis, jnp.exp(s - m), 0.0)
    p = e / jnp.maximum(jnp.sum(e, axis=-1, keepdims=True), TINY)
    o = _dot(p.astype(bf16), vc)
    imp = jnp.sum(p.reshape(g, tq, nb), axis=0)
    col = lax.broadcasted_iota(i32, (tq, nb), 1)
    colf = col.astype(f32)
    forced = (col == 0) | (col == cur) | (col == cur - 1)
    imp = jnp.where(forced, FORCE_SCORE, imp)
    imp = jnp.where(col <= cur, imp, NEG)
    sel = jnp.zeros((tq, nb), f32)
    for _ in range(N_SEL):
        mx = jnp.max(imp, axis=-1, keepdims=True)
        idx = jnp.min(jnp.where(imp == mx, colf, float(nb)), axis=-1, keepdims=True)
        hit = colf == idx
        sel = jnp.where(hit, 1.0, sel)
        imp = jnp.where(hit, -jnp.inf, imp)
    return o, sel


def _cmp_prompt_kernel(qt_ref, kc_ref, vct_ref, bc_ref, c31_ref, o_ref, sel_ref, *, g, tq):
    hd, cols = qt_ref.shape
    nb = kc_ref.shape[0]
    qi = pl.program_id(1)
    blk = lax.broadcasted_iota(i32, (nb, cols), 0)
    first = (tq // BLK) * qi - 2
    bias = jnp.where(blk < first, c31_ref[...], NEG)
    for jj in range(bc_ref.shape[0]):
        bias = jnp.where(blk == first + jj, bc_ref[jj], bias)
    vis = bias > 0.5 * NEG
    s = _dot(kc_ref[...], qt_ref[...]) * (hd ** -0.5) + bias
    m = jnp.max(s, axis=0, keepdims=True)
    e = jnp.where(vis, jnp.exp(s - m), 0.0)
    p = e / jnp.maximum(jnp.sum(e, axis=0, keepdims=True), TINY)
    o_ref[...] = _dot(vct_ref[...], p.astype(bf16))

    imp = p[:, :tq]
    for gi in range(1, g):
        imp = imp + p[:, gi * tq:(gi + 1) * tq]
    blk = lax.broadcasted_iota(i32, (nb, tq), 0)
    blkf = blk.astype(f32)
    cur = (qi * tq + lax.broadcasted_iota(i32, (1, tq), 1)) // BLK
    forced = (blk == 0) | (blk == cur) | (blk == cur - 1)
    imp = jnp.where(forced, FORCE_SCORE, imp)
    imp = jnp.where(blk <= cur, imp, NEG)
    sel = jnp.zeros((nb, tq), f32)
    for _ in range(N_SEL):
        mx = jnp.max(imp, axis=0, keepdims=True)
        idx = jnp.min(jnp.where(imp == mx, blkf, float(nb)), axis=0, keepdims=True)
        hit = blkf == idx
        sel = jnp.where(hit, 1.0, sel)
        imp = jnp.where(hit, -jnp.inf, imp)
    sel_ref[...] = sel.astype(bf16)


def _cmp_prompt(qt, kc, vct, bc, c31, g, tq):
    _, nqt, hd, cols = qt.shape
    nb = kc.shape[1]
    nband = bc.shape[1]
    return pl.pallas_call(
        functools.partial(_cmp_prompt_kernel, g=g, tq=tq),
        out_shape=(jax.ShapeDtypeStruct((KV, nqt, hd, cols), f32),
                   jax.ShapeDtypeStruct((KV, nb, nqt * tq), bf16)),
        grid=(KV, nqt),
        in_specs=[pl.BlockSpec((None, None, hd, cols), lambda h, i: (h, i, 0, 0)),
                  pl.BlockSpec((None, nb, hd), lambda h, i: (h, 0, 0)),
                  pl.BlockSpec((None, hd, nb), lambda h, i: (h, 0, 0)),
                  pl.BlockSpec((None, nband, 1, cols), lambda h, i: (h, 0, 0, 0)),
                  pl.BlockSpec((None, 1, cols), lambda h, i: (h, 0, 0))],
        out_specs=(pl.BlockSpec((None, None, hd, cols), lambda h, i: (h, i, 0, 0)),
                   pl.BlockSpec((None, nb, tq), lambda h, i: (h, 0, i))),
        compiler_params=_cparams(("arbitrary", "arbitrary")),
        name="cmp_prompt",
    )(qt, kc, vct, bc, c31)


def _cmp_sample_kernel(q_ref, kc_ref, vc_ref, b_ref, cur_ref, o_ref, sel_ref):
    ts = cur_ref.shape[0]
    g = q_ref.shape[1] // ts
    for h in range(KV):
        o, sel = _cmp_core(q_ref[h], kc_ref[h], vc_ref[h], b_ref[h], cur_ref[...], g, ts)
        o_ref[h] = o
        sel_ref[h] = sel.astype(bf16)


def _cmp_sample(q, kc, vc, bias, cur):
    db, _, r, hd = q.shape
    nbp = kc.shape[2]
    ts = cur.shape[0]
    return pl.pallas_call(
        _cmp_sample_kernel,
        out_shape=(jax.ShapeDtypeStruct((db, KV, r, hd), f32), jax.ShapeDtypeStruct((db, KV, ts, nbp), bf16)),
        grid=(db,),
        in_specs=[pl.BlockSpec((None, KV, r, hd), lambda b: (b, 0, 0, 0)),
                  pl.BlockSpec((None, KV, nbp, hd), lambda b: (b, 0, 0, 0)),
                  pl.BlockSpec((None, KV, nbp, hd), lambda b: (b, 0, 0, 0)),
                  pl.BlockSpec((KV, r, nbp), lambda b: (0, 0, 0)),
                  pl.BlockSpec((ts, 1), lambda b: (0, 0))],
        out_specs=(pl.BlockSpec((None, KV, r, hd), lambda b: (b, 0, 0, 0)),
                   pl.BlockSpec((None, KV, ts, nbp), lambda b: (b, 0, 0, 0))),
        compiler_params=_cparams(("arbitrary",)),
        name="cmp_sample",
    )(q, kc, vc, bias, cur)


def _softmax_chunk(a, vt_dot, m_ref, l_ref, acc_ref, rows):
    m_old = m_ref[rows, :]
    m_new = jnp.maximum(m_old, jnp.max(a, axis=-1, keepdims=True))
    alpha = jnp.exp(m_old - m_new)
    e = jnp.exp(a - m_new)
    l_ref[rows, :] = alpha * l_ref[rows, :] + jnp.sum(e, axis=-1, keepdims=True)
    acc_ref[rows, :] = alpha * acc_ref[rows, :] + vt_dot(e.astype(bf16))
    m_ref[rows, :] = m_new


def _slc_prompt_kernel(q_ref, kt_ref, v_ref, sel_ref, ex_ref, b_ref, o_ref,
                       s_scr, p_scr, madd_scr, m_scr, mt_scr, al_scr, acc_scr):
    g, tq, hd = q_ref.shape
    qi = pl.program_id(1)
    rows = g * tq
    rc = SLC_ROWS
    nchunk = rows // rc
    per_head = tq // rc

    def tile(kstart, width, bias_col):
        kt = kt_ref[:, pl.ds(kstart, width)]
        vt = v_ref[pl.ds(kstart, width), :]
        s_scr[:, :width] = _dot(q_ref[...].reshape(rows, hd), kt)
        msk = _dot(sel_ref[...], ex_ref[:, pl.ds(kstart, width)])
        madd_scr[:, :width] = jnp.where(msk > 0.5, 0.0, NEG)

        def logits(ci):
            part = ci % per_head
            a = s_scr[ci * rc:(ci + 1) * rc, :width] + madd_scr[part * rc:(part + 1) * rc, :width]
            if bias_col is not None:
                a = a + b_ref[ci * rc:(ci + 1) * rc, bias_col:bias_col + width]
            return a

        for ci in range(nchunk):
            mt_scr[ci * rc:(ci + 1) * rc, :] = jnp.broadcast_to(
                jnp.max(logits(ci), axis=-1, keepdims=True), (rc, LANES))
        m_old = m_scr[...]
        m_new = jnp.maximum(m_old, mt_scr[...])
        al_scr[...] = jnp.exp(m_old - m_new)
        m_scr[...] = m_new
        for ci in range(nchunk):
            m_c = m_scr[ci * rc:(ci + 1) * rc, :]
            e = jnp.exp(logits(ci) - jnp.tile(m_c, (1, width // LANES)))
            p_scr[ci * rc:(ci + 1) * rc, :width] = e.astype(bf16)
        acc_scr[...] = al_scr[...] * acc_scr[...] + _dot(p_scr[:, :width], vt)

    m_scr[...] = jnp.full(m_scr.shape, NEG, f32)
    acc_scr[...] = jnp.zeros(acc_scr.shape, f32)

    @pl.when(qi == 0)
    def _():
        tile(0, tq, tq)

    @pl.when(qi > 0)
    def _():
        tile(pl.multiple_of((qi - 1) * tq, tq), 2 * tq, 0)

    nfar = jnp.maximum(qi - 1, 0)

    def far(j, carry):
        tile(pl.multiple_of(j * 2 * tq, 2 * tq), 2 * tq, None)
        return carry
    lax.fori_loop(0, nfar // 2, far, 0)

    @pl.when(nfar % 2 == 1)
    def _():
        tile(pl.multiple_of((nfar - 1) * tq, tq), tq, None)

    acc = acc_scr[...]
    o_ref[...] = (acc[:, :hd] / acc[:, hd:hd + 1]).reshape(g, tq, hd)


def _slc_prompt(q, kt, v1, sel, expand, bias, tq):
    _, g, t, hd = q.shape
    nb = sel.shape[-1]
    rows = g * tq
    return pl.pallas_call(
        _slc_prompt_kernel,
        out_shape=jax.ShapeDtypeStruct((KV, g, t, hd), f32),
        grid=(KV, t // tq),
        in_specs=[pl.BlockSpec((None, g, tq, hd), lambda h, i: (h, 0, i, 0)),
                  pl.BlockSpec((None, hd, t), lambda h, i: (h, 0, 0)),
                  pl.BlockSpec((None, t, LANES), lambda h, i: (h, 0, 0)),
                  pl.BlockSpec((None, tq, nb), lambda h, i: (h, i, 0)),
                  pl.BlockSpec((nb, t), lambda h, i: (0, 0)),
                  pl.BlockSpec((None, rows, 2 * tq), lambda h, i: (h, 0, 0))],
        out_specs=pl.BlockSpec((None, g, tq, hd), lambda h, i: (h, 0, i, 0)),
        scratch_shapes=[pltpu.VMEM((rows, 2 * tq), f32), pltpu.VMEM((rows, 2 * tq), bf16),
                        pltpu.VMEM((tq, 2 * tq), f32), pltpu.VMEM((rows, LANES), f32),
                        pltpu.VMEM((rows, LANES), f32), pltpu.VMEM((rows, LANES), f32),
                        pltpu.VMEM((rows, LANES), f32)],
        compiler_params=_cparams(("arbitrary", "arbitrary"), VMEM_LIMIT),
        name="slc_prompt",
    )(q, kt, v1, sel, expand, bias)


def _slc_sample_kernel(pt_ref, pool_ref, q_ref, sel_ref, ex_ref, kn_ref, vn_ref, bl_ref, bn_ref, exn_ref, o_ref,
                       buf, sem, madd_ref, m_ref, l_ref, acc_ref, *, ch0, pgs, nchunk, ppt):
    b = pl.program_id(0)
    c = pl.program_id(1)
    step = b * nchunk + c
    nsteps = pl.num_programs(0) * nchunk
    slot = step % 2
    psz = buf.shape[-1]
    rows = q_ref.shape[0]
    ntile = pgs // ppt
    tk = ppt * psz
    last = c == nchunk - 1

    def page_copy(bb, cc, p, sl):
        page = pt_ref[bb, cc * pgs + p]
        return pltpu.make_async_copy(pool_ref.at[page, pl.ds(ch0, 2)], buf.at[sl, p], sem.at[sl])

    def start(st, sl):
        bb = st // nchunk
        cc = st % nchunk

        def body(p, carry):
            page_copy(bb, cc, p, sl).start()
            return carry
        lax.fori_loop(0, pgs, body, 0)

    @pl.when(step == 0)
    def _():
        start(step, slot)

    @pl.when(step + 1 < nsteps)
    def _():
        start(step + 1, 1 - slot)

    def wait_body(p, carry):
        page_copy(b, c, p, slot).wait()
        return carry
    lax.fori_loop(0, pgs, wait_body, 0)

    @pl.when(c == 0)
    def _():
        m_ref[...] = jnp.full(m_ref.shape, NEG, f32)
        l_ref[...] = jnp.zeros(l_ref.shape, f32)
        acc_ref[...] = jnp.zeros(acc_ref.shape, f32)

    madd_ref[...] = jnp.where(_dot(sel_ref[...], ex_ref[...]) > 0.5, 0.0, NEG)

    def tile(j, with_bias):
        kt = jnp.concatenate([buf[slot, j * ppt + p, 0] for p in range(ppt)], axis=1).astype(bf16)
        vt = jnp.concatenate([buf[slot, j * ppt + p, 1] for p in range(ppt)], axis=1).astype(bf16)
        col0 = pl.multiple_of(j * tk, tk)
        a = _dot(q_ref[...], kt) + madd_ref[:, pl.ds(col0, tk)]
        if with_bias:
            a = a + bl_ref[...]
        _softmax_chunk(a, lambda p: _dot_nt(p, vt), m_ref, l_ref, acc_ref, slice(None))

    def plain(j, carry):
        tile(j, False)
        return carry
    lax.fori_loop(0, ntile - jnp.where(last, 1, 0), plain, 0)

    @pl.when(last)
    def _():
        tile(ntile - 1, True)
        madd_new = jnp.where(_dot(sel_ref[...], exn_ref[...]) > 0.5, 0.0, NEG)
        vn = vn_ref[...]
        a = _dot(q_ref[...], kn_ref[...]) + madd_new + bn_ref[...]
        _softmax_chunk(a, lambda p: _dot_nt(p, vn), m_ref, l_ref, acc_ref, slice(None))
        full = acc_ref[...] / l_ref[...]
        r = rows // KV
        o_ref[...] = jnp.concatenate(
            [full[h * r:(h + 1) * r, h * HEAD_DIM:(h + 1) * HEAD_DIM] for h in range(KV)], axis=0)


def _slc_sample(pool, ch0, pt, qbd, selr, expand, knew, vnew, bias_last, bias_new, expand_new, ppt=4):
    db, npg = pt.shape
    psz = pool.shape[-1]
    rows = qbd.shape[1]
    nkd = KV * HEAD_DIM
    pgs = min(npg, 32)
    nchunk = npg // pgs
    ppt = min(ppt, pgs)
    nbp = selr.shape[-1]
    ck = pgs * psz
    kern = functools.partial(_slc_sample_kernel, ch0=ch0, pgs=pgs, nchunk=nchunk, ppt=ppt)
    return pl.pallas_call(
        kern,
        out_shape=jax.ShapeDtypeStruct((db, rows, HEAD_DIM), f32),
        grid_spec=pltpu.PrefetchScalarGridSpec(
            num_scalar_prefetch=1,
            grid=(db, nchunk),
            in_specs=[pl.BlockSpec(memory_space=pl.ANY),
                      pl.BlockSpec((None, rows, nkd), lambda b, c, pt: (b, 0, 0)),
                      pl.BlockSpec((None, rows, nbp), lambda b, c, pt: (b, 0, 0)),
                      pl.BlockSpec((nbp, ck), lambda b, c, pt: (0, c)),
                      pl.BlockSpec((None, nkd, BLK), lambda b, c, pt: (b, 0, 0)),
                      pl.BlockSpec((None, nkd, BLK), lambda b, c, pt: (b, 0, 0)),
                      pl.BlockSpec((rows, ppt * psz), lambda b, c, pt: (0, 0)),
                      pl.BlockSpec((rows, BLK), lambda b, c, pt: (0, 0)),
                      pl.BlockSpec((nbp, BLK), lambda b, c, pt: (0, 0))],
            out_specs=pl.BlockSpec((None, rows, HEAD_DIM), lambda b, c, pt: (b, 0, 0)),
            scratch_shapes=[pltpu.VMEM((2, pgs, 2, nkd, psz), f32),
                            pltpu.SemaphoreType.DMA((2,)),
                            pltpu.VMEM((rows, ck), f32),
                            pltpu.VMEM((rows, 1), f32), pltpu.VMEM((rows, 1), f32),
                            pltpu.VMEM((rows, nkd), f32)]),
        compiler_params=_cparams(("arbitrary", "arbitrary"), VMEM_LIMIT),
        name="slc_sample",
    )(pt, pool, qbd, selr, expand, knew, vnew, bias_last, bias_new, expand_new)


def _merge_kernel(gl_ref, oc_ref, os_ref, ow_ref, ex_ref, o_ref):
    gate = jax.nn.sigmoid(gl_ref[...])
    hi = gate.astype(bf16)
    lo = (gate - hi.astype(f32)).astype(bf16)
    acc = None
    for br, ref in enumerate((oc_ref, os_ref, ow_ref)):
        ex = ex_ref[br]
        gx = _dot(hi, ex) + _dot(lo, ex)
        term = gx * ref[...]
        acc = term if acc is None else acc + term
    o_ref[...] = acc.astype(bf16)


def _merge(glog, oc, osl, ow, expand, tm=TOKEN_TILE):
    t, n = oc.shape
    return pl.pallas_call(
        _merge_kernel,
        out_shape=jax.ShapeDtypeStruct((t, n), bf16),
        grid=(t // tm,),
        in_specs=[pl.BlockSpec((tm, LANES), lambda i: (i, 0)),
                  pl.BlockSpec((tm, n), lambda i: (i, 0)),
                  pl.BlockSpec((tm, n), lambda i: (i, 0)),
                  pl.BlockSpec((tm, n), lambda i: (i, 0)),
                  pl.BlockSpec((3, LANES, n), lambda i: (0, 0, 0))],
        out_specs=pl.BlockSpec((tm, n), lambda i: (i, 0)),
        compiler_params=_cparams(("arbitrary",)),
        name="nsa_merge",
    )(glog, oc, osl, ow, expand)


def _moe_kernel(te_ref, ns_ref, ic_ref, in_ref, h_ref, wg_ref, wl_ref, bg_ref, bl_ref, wd_ref, bd_ref, o_ref,
                xbuf, sem, xs, wg_s, wl_s, wd_s, *, sub):
    t = pl.program_id(0)
    f = pl.program_id(1)
    nt = pl.num_programs(0)
    nsub = ns_ref[t]

    half = xbuf.shape[1]
    unroll = 8

    def row_copy(idx_ref, r):
        return pltpu.make_async_copy(h_ref.at[pl.ds(idx_ref[0, r], 1)], xbuf.at[pl.ds(r, 1)], sem.at[0])

    def for_rows(nrows, fn):
        def body(i, carry):
            for u in range(unroll):
                fn(i * unroll + u)
            return carry
        lax.fori_loop(0, nrows // unroll, body, 0)

    def start(idx_ref, nrows):
        for_rows(nrows, lambda r: row_copy(idx_ref, r).start())

    @pl.when(f == 0)
    def _():
        o_ref[...] = jnp.zeros(o_ref.shape, f32)

        @pl.when(jnp.logical_and(t == 0, nsub > 0))
        def _():
            start(ic_ref, nsub * sub)

        @pl.when(nsub > 0)
        def _():
            for_rows(nsub * sub, lambda r: row_copy(ic_ref, r).wait())

            def unpack_body(s, carry):
                r = pl.multiple_of(s * sub, sub)
                w = xbuf[pl.ds(r, sub), :]
                lo = lax.bitcast_convert_type(lax.shift_left(w, jnp.uint32(16)), f32)
                hi = lax.bitcast_convert_type(w & jnp.uint32(0xFFFF0000), f32)
                xs[pl.ds(r, sub), :half] = lo.astype(bf16)
                xs[pl.ds(r, sub), half:] = hi.astype(bf16)
                return carry
            lax.fori_loop(0, nsub, unpack_body, 0)

        nnext = ns_ref[jnp.minimum(t + 1, nt - 1)]

        @pl.when(jnp.logical_and(t + 1 < nt, nnext > 0))
        def _():
            start(in_ref, nnext * sub)

    @pl.when(nsub > 0)
    def _():
        wg_s[...] = wg_ref[...].astype(bf16)
        wl_s[...] = wl_ref[...].astype(bf16)
        wd_s[...] = wd_ref[...].astype(bf16)
        first = jnp.where(f == 0, 1.0, 0.0)

        def chunk(r, rows):
            x = xs[pl.ds(r, rows), :]
            glu = jnp.minimum(_dot(x, wg_s[...]) + bg_ref[...], SWIGLU_LIMIT)
            lin = jnp.clip(_dot(x, wl_s[...]) + bl_ref[...], -SWIGLU_LIMIT, SWIGLU_LIMIT)
            act = glu * jax.nn.sigmoid(SWIGLU_ALPHA * glu) * (lin + 1.0)
            y = _dot(act.astype(bf16), wd_s[...])
            o_ref[pl.ds(r, rows), :] += y + first * bd_ref[...]

        def quad(j, carry):
            chunk(pl.multiple_of(j * 4 * sub, 4 * sub), 4 * sub)
            return carry
        nquad = nsub // 4
        lax.fori_loop(0, nquad, quad, 0)
        rem = nsub - 4 * nquad

        @pl.when(rem >= 2)
        def _():
            chunk(pl.multiple_of(nquad * 4 * sub, 2 * sub), 2 * sub)

        @pl.when(rem % 2 == 1)
        def _():
            chunk(pl.multiple_of((nsub - 1) * sub, sub), sub)


def _moe_experts(h, src_tok, tile_e, tile_ns, layer, w_up, b_up, w_down, b_down):
    t = h.shape[0]
    d = 2 * h.shape[1]
    nt, _, tm = src_tok.shape
    ff = w_down.shape[2]
    fc = min(MOE_FC, ff)
    nf = ff // fc

    def fidx(ti, f, ns):
        return jnp.where(ns[ti] > 0, f, 0)

    idx_c = pl.BlockSpec((None, 1, tm), lambda ti, f, te, ns: (ti, 0, 0), memory_space=pltpu.SMEM)
    idx_n = pl.BlockSpec((None, 1, tm), lambda ti, f, te, ns: (jnp.minimum(ti + 1, nt - 1), 0, 0),
                         memory_space=pltpu.SMEM)
    return pl.pallas_call(
        functools.partial(_moe_kernel, sub=MOE_SUB),
        out_shape=jax.ShapeDtypeStruct((nt * tm, d), f32),
        grid_spec=pltpu.PrefetchScalarGridSpec(
            num_scalar_prefetch=2,
            grid=(nt, nf),
            in_specs=[idx_c, idx_n,
                      pl.BlockSpec(memory_space=pl.ANY),
                      pl.BlockSpec((None, None, d, fc), lambda ti, f, te, ns: (layer, te[ti], 0, fidx(ti, f, ns))),
                      pl.BlockSpec((None, None, d, fc),
                                   lambda ti, f, te, ns: (layer, te[ti], 0, nf + fidx(ti, f, ns))),
                      pl.BlockSpec((None, None, 1, fc), lambda ti, f, te, ns: (layer, te[ti], 0, fidx(ti, f, ns))),
                      pl.BlockSpec((None, None, 1, fc),
                                   lambda ti, f, te, ns: (layer, te[ti], 0, nf + fidx(ti, f, ns))),
                      pl.BlockSpec((None, None, fc, d), lambda ti, f, te, ns: (layer, te[ti], fidx(ti, f, ns), 0)),
                      pl.BlockSpec((None, None, 1, d), lambda ti, f, te, ns: (layer, te[ti], 0, 0))],
            out_specs=pl.BlockSpec((tm, d), lambda ti, f, te, ns: (ti, 0)),
            scratch_shapes=[pltpu.VMEM((tm, d // 2), jnp.uint32), pltpu.SemaphoreType.DMA((1,)),
                            pltpu.VMEM((tm, d), bf16),
                            pltpu.VMEM((d, fc), bf16), pltpu.VMEM((d, fc), bf16), pltpu.VMEM((fc, d), bf16)]),
        compiler_params=_cparams(("arbitrary", "arbitrary"), VMEM_LIMIT),
        name="moe_experts",
    )(tile_e, tile_ns, src_tok, src_tok, h, w_up, w_up, b_up, b_up, w_down, b_down)


def _combine_kernel(ic_ref, in_ref, y_ref, x_ref, tw_ref, gt_ref, o_ref, buf, sem):
    step = pl.program_id(0)
    nsteps = pl.num_programs(0)
    slot = step % 2
    nrow = buf.shape[1]

    unroll = 8

    def row_copy(idx_ref, r, sl):
        return pltpu.make_async_copy(y_ref.at[pl.ds(idx_ref[0, r], 1)], buf.at[sl, pl.ds(r, 1)], sem.at[sl])

    def for_rows(fn):
        def body(i, carry):
            for u in range(unroll):
                fn(i * unroll + u)
            return carry
        lax.fori_loop(0, nrow // unroll, body, 0)

    @pl.when(step == 0)
    def _():
        for_rows(lambda r: row_copy(ic_ref, r, slot).start())

    @pl.when(step + 1 < nsteps)
    def _():
        for_rows(lambda r: row_copy(in_ref, r, 1 - slot).start())

    for_rows(lambda r: row_copy(ic_ref, r, slot).wait())

    tt, d = x_ref.shape
    rg = gt_ref.shape[0]
    tw = tw_ref[...]
    acc = None
    for k in range(TOP_K):
        term = tw[:, k:k + 1] * buf[slot, pl.ds(k * tt, tt), :]
        acc = term if acc is None else acc + term
    if tt >= rg:
        upd = (acc.reshape(tt // rg, rg, d) * gt_ref[...][None]).reshape(tt, d)
    else:
        part = step % (rg // tt)
        upd = acc * gt_ref[pl.ds(pl.multiple_of(part * tt, tt), tt), :]
    o_ref[...] = x_ref[...] + upd


def _moe_combine(y, pos, x, topw, mods, m_idx, tp):
    t, d = x.shape
    tt = MOE_TT
    nsteps = t // tt
    rg = mods.shape[1] // 2
    npt = tp // tt
    n_idx = TOP_K * tt
    idx_c = pl.BlockSpec((None, 1, n_idx), lambda i: (i, 0, 0), memory_space=pltpu.SMEM)
    idx_n = pl.BlockSpec((None, 1, n_idx), lambda i: (jnp.minimum(i + 1, nsteps - 1), 0, 0),
                         memory_space=pltpu.SMEM)
    return pl.pallas_call(
        _combine_kernel,
        out_shape=jax.ShapeDtypeStruct((t, d), f32),
        grid=(nsteps,),
        in_specs=[idx_c, idx_n,
                  pl.BlockSpec(memory_space=pl.ANY),
                  pl.BlockSpec((tt, d), lambda i: (i, 0)),
                  pl.BlockSpec((tt, LANES), lambda i: (i, 0)),
                  pl.BlockSpec((None, rg, d), lambda i: (m_idx, jnp.where(i < npt, 0, 1), 2))],
        out_specs=pl.BlockSpec((tt, d), lambda i: (i, 0)),
        scratch_shapes=[pltpu.VMEM((2, n_idx, d), f32), pltpu.SemaphoreType.DMA((2,))],
        compiler_params=_cparams(("arbitrary",)),
        name="moe_combine",
    )(pos, pos, y, x, topw, mods)


def _moe_layer(x, g, mods, m_idx, tp, layer, router_w, router_b, w_up, b_up, w_down, b_down):
    t, d = x.shape
    ne = router_w.shape[1]
    ff = w_down.shape[2]
    tm, sub, tt = MOE_TM, MOE_SUB, MOE_TT

    rw = jnp.pad(router_w, ((0, 0), (0, LANES - ne)))
    whi = rw.astype(bf16)
    wlo = (rw - whi.astype(f32)).astype(bf16)
    rb = jnp.pad(router_b.reshape(1, ne), ((0, 0), (0, LANES - ne)), constant_values=-jnp.inf)
    h, topi, topw = _modnorm(x, g, mods, m_idx, tp, router=(whi, wlo, rb))

    e_flat = topi[:, :TOP_K].reshape(-1)
    npair = t * TOP_K
    onehot = (e_flat[:, None] == jnp.arange(ne, dtype=i32)[None, :]).astype(i32)
    counts = jnp.sum(onehot, axis=0)
    order = jnp.argsort(e_flat, stable=True).astype(i32)
    inv = jnp.argsort(order).astype(i32)
    tiles_e = (counts + tm - 1) // tm
    cum_tiles = jnp.cumsum(tiles_e)
    tile_start = cum_tiles - tiles_e
    grp_start = jnp.cumsum(counts) - counts
    nt = npair // tm + ne
    tile_ids = jnp.arange(nt, dtype=i32)
    tile_e = jnp.clip(jnp.searchsorted(cum_tiles, tile_ids, side="right"), 0, ne - 1).astype(i32)
    tile_off = (tile_ids - tile_start[tile_e]) * tm
    rows_in_tile = jnp.where(tile_ids < cum_tiles[-1], jnp.clip(counts[tile_e] - tile_off, 0, tm), 0)
    tile_ns = ((rows_in_tile + sub - 1) // sub).astype(i32)
    pad_e = tiles_e * tm - counts
    pad_key = jnp.where(jnp.arange(tm, dtype=i32)[None, :] < pad_e[:, None], jnp.arange(ne, dtype=i32)[:, None], ne)
    keys = jnp.concatenate([e_flat, pad_key.reshape(-1)])
    toks = jnp.concatenate([jnp.arange(npair, dtype=i32) // TOP_K, jnp.zeros((ne * tm,), i32)])
    src_tok = lax.sort_key_val(keys, toks, is_stable=True)[1][:nt * tm]
    pair_tile_start = jnp.sum(onehot * tile_start[None, :], axis=1)
    pair_grp_start = jnp.sum(onehot * grp_start[None, :], axis=1)
    pos = pair_tile_start * tm + (inv - pair_grp_start)

    y = _moe_experts(h, src_tok.reshape(nt, 1, tm).astype(i32), tile_e, tile_ns, layer,
                     w_up, b_up.reshape(b_up.shape[0], ne, 1, 2 * ff), w_down,
                     b_down.reshape(b_down.shape[0], ne, 1, d))
    pos_steps = pos.astype(i32).reshape(t // tt, tt, TOP_K).transpose(0, 2, 1).reshape(t // tt, 1, TOP_K * tt)
    return _moe_combine(y, pos_steps, x, topw, mods, m_idx, tp)


def _rms_kernel(x_ref, g_ref, o_ref):
    x = x_ref[...]
    o_ref[...] = x * lax.rsqrt(jnp.mean(x * x, axis=-1, keepdims=True) + EPS) * g_ref[...]


def _rmsnorm(x, g, tm=TOKEN_TILE):
    t, d = x.shape
    return pl.pallas_call(
        _rms_kernel,
        out_shape=jax.ShapeDtypeStruct((t, d), f32),
        grid=(t // tm,),
        in_specs=[pl.BlockSpec((tm, d), lambda i: (i, 0)), pl.BlockSpec((1, d), lambda i: (0, 0))],
        out_specs=pl.BlockSpec((tm, d), lambda i: (i, 0)),
        compiler_params=_cparams(("arbitrary",)),
        name="final_norm",
    )(x, g)


def _heads_prompt(a, tp):
    return a[:tp].reshape(tp, KV, G, HEAD_DIM).transpose(1, 2, 0, 3)


def _heads_sample(a, tp, ts, db):
    return a[tp:].reshape(ts, db, KV, G, HEAD_DIM).transpose(1, 2, 3, 0, 4).reshape(db, KV, G * ts, HEAD_DIM)


def _unheads(op, osm, tp, ts, db):
    a = op.transpose(2, 0, 1, 3).reshape(tp, N_HEADS * HEAD_DIM)
    b = osm.reshape(db, KV, G, ts, HEAD_DIM).transpose(3, 0, 1, 2, 4).reshape(ts * db, N_HEADS * HEAD_DIM)
    return jnp.concatenate([a, b], axis=0)


def _kv_prompt(a, tp):
    return a[:tp].reshape(tp, KV, HEAD_DIM).transpose(1, 0, 2)


def _kv_sample(a, tp, ts, db):
    return a[tp:].reshape(ts, db, KV, HEAD_DIM).transpose(1, 0, 2, 3)


def _pages_dim_major(rows, psz):
    n = rows.shape[0] // psz
    return rows.reshape(n, psz, -1, KV, HEAD_DIM).transpose(0, 2, 3, 4, 1)


def _block_expand(nb, nkeys, first_block=0):
    blk = first_block + jnp.arange(nkeys, dtype=i32) // BLK
    return (jnp.arange(nb, dtype=i32)[:, None] == blk[None, :]).astype(bf16)


def kernel(x_prompt, x_sample, state_swa, cache_nsa, state_nsa_win, page_table, c_prompt, c_sample,
           rel_table, w_qkv_a, sink_a, w_o_a, w_q_b, w_gate_b, w_o_b, w_kv_shared, cmp_pe, cmp_w1, cmp_w2,
           router_w, router_b, w_up, b_up, w_down, b_down, norm_g, ada_w, ada_b,
           kv_norm_g, ada_kv_w, ada_kv_b, final_norm_g):
    bsz, tp, d = x_prompt.shape
    db, ts, _ = x_sample.shape
    n_a = w_qkv_a.shape[0]
    depth = norm_g.shape[0]
    npg = page_table.shape[1]
    psz = cache_nsa.shape[1]
    past = npg * psz
    wb = state_swa.shape[2]
    nq = N_HEADS * HEAD_DIM
    nk = KV * HEAD_DIM
    scale = HEAD_DIM ** -0.5
    assert bsz == 1 and wb == WINDOW and state_nsa_win.shape[1] == WINDOW
    assert psz == 2 * BLK and past % BLK == 0 and ts <= BLK and tp % SLC_TQ == 0 and tp % TOKEN_TILE == 0
    t = tp + ts * db

    x = jnp.concatenate([x_prompt[0], x_sample.transpose(1, 0, 2).reshape(ts * db, d)], axis=0)
    cmat = jnp.concatenate([jnp.broadcast_to(c_prompt, (db, d)), c_sample], axis=0)
    mods = _ada(cmat, ada_w.reshape(depth * 2, d, 3 * d), ada_b.reshape(depth * 2, 1, 3 * d))
    mods_kv = _ada(cmat, ada_kv_w[None], ada_kv_b[None, None])

    lut = rel_table[_rel_bucket(jnp.arange(LUT_N))].astype(f32)
    c31 = lut[LUT_N - 1]
    bias_band = _toeplitz_bias(lut, WINDOW, 2 * WINDOW, WINDOW, window=WINDOW)
    bias_step = _toeplitz_bias(lut, ts, wb + ts, wb, window=WINDOW)

    swa_p, swa_s = [], []
    for layer in range(depth):
        if layer == n_a:
            h_kv = _modnorm(x, kv_norm_g.reshape(1, d), mods_kv, 0, tp)
            kvr = _mm(h_kv, w_kv_shared.astype(bf16))
            kv_s = kvr[tp:].reshape(ts, db, 6 * nk).transpose(1, 0, 2)
            nsa_rows_p = kvr[:tp, :4 * nk].reshape(1, tp, 4, KV, HEAD_DIM)
            nsa_rows_s = kv_s[:, :, :4 * nk].reshape(db, ts, 4, KV, HEAD_DIM)
            nsa_win_p = kvr[tp - WINDOW:tp, 4 * nk:].reshape(1, WINDOW, 2, KV, HEAD_DIM)
            nsa_win_s = jnp.concatenate(
                [state_nsa_win, kv_s[:, :, 4 * nk:].reshape(db, ts, 2, KV, HEAD_DIM)], axis=1)[:, -WINDOW:]

            nhid = cmp_w1.shape[-1]
            pe_t = jnp.tile(cmp_pe.transpose(0, 2, 1), (1, 1, 2)).reshape(2, HEAD_DIM, 1, psz)
            w1t = cmp_w1.reshape(2, BLK, HEAD_DIM // 2, 2, nhid).transpose(0, 2, 3, 1, 4)
            eye2 = jnp.eye(2, dtype=f32)
            w1p = (w1t[:, :, :, None, :, None, :] * eye2[None, None, None, :, None, :, None]).reshape(
                2, HEAD_DIM // 2, 2 * psz, 2 * nhid).astype(bf16)
            w2b = cmp_w2.astype(bf16)
            npg_p = tp // psz
            cmp_p = _compress(_pages_dim_major(kvr[:tp, :2 * nk], psz), 0,
                              jnp.arange(npg_p, dtype=i32)[None], pe_t, w1p, w2b)[0]
            pool = cache_nsa.transpose(0, 2, 3, 4, 1)
            cmp_s = _compress(pool, 0, page_table, pe_t, w1p, w2b)
            tail = jnp.pad(kv_s[:, :, :2 * nk], ((0, 0), (0, psz - ts), (0, 0))).reshape(db * psz, 2 * nk)
            cmp_t = _compress(_pages_dim_major(tail, psz), 0, jnp.arange(db, dtype=i32)[None], pe_t, w1p, w2b)[0]
            cmp_t = cmp_t.reshape(2, KV, db, 2, HEAD_DIM)[:, :, :, 0].transpose(2, 0, 1, 3)
            nb_s = past // BLK + 1
            nbp = ((nb_s + 255) // 256) * 256
            cmp_s = jnp.concatenate(
                [cmp_s, cmp_t[:, :, :, None], jnp.zeros((db, 2, KV, nbp - nb_s, HEAD_DIM), f32)], axis=3)
            kc_p, vc_p = cmp_p[0].astype(bf16), cmp_p[1].astype(bf16)
            kc_s, vc_s = cmp_s[:, 0].astype(bf16), cmp_s[:, 1].astype(bf16)

        g_mix = norm_g[layer, 0].reshape(1, d)
        h = _modnorm(x, g_mix, mods, 2 * layer, tp)
        if layer < n_a:
            y = _mm(h, w_qkv_a[layer].astype(bf16))
            q, k, v = y[:, :nq], y[:, nq:nq + nk], y[:, nq + nk:]
            sink = sink_a[layer].astype(f32)
            o_p = _band_attn(_heads_prompt(q, tp).astype(bf16), _kv_prompt(k, tp).astype(bf16),
                             _kv_prompt(v, tp).astype(bf16), bias_band, sink)
            k_s, v_s = _kv_sample(k, tp, ts, db), _kv_sample(v, tp, ts, db)
            kk = jnp.concatenate([state_swa[layer, :, :, 0], k_s], axis=1).transpose(0, 2, 1, 3)
            vv = jnp.concatenate([state_swa[layer, :, :, 1], v_s], axis=1).transpose(0, 2, 1, 3)
            o_s = _step_attn(_heads_sample(q, tp, ts, db).astype(bf16), kk.astype(bf16), vv.astype(bf16),
                             bias_step, _head_rows(sink, ts))
            swa_p.append(jnp.stack([k[tp - wb:tp].reshape(1, wb, KV, HEAD_DIM),
                                    v[tp - wb:tp].reshape(1, wb, KV, HEAD_DIM)], axis=2))
            swa_s.append(jnp.concatenate([state_swa[layer], jnp.stack([k_s, v_s], axis=2)], axis=1)[:, -wb:])
            x = _mm_res(_unheads(o_p, o_s, tp, ts, db), w_o_a[layer].astype(bf16), x, mods, 2 * layer, tp)
        else:
            j = layer - n_a
            qf = _mm(h, w_q_b[j].astype(bf16))
            ngate = 3 * N_HEADS
            glog = _mm(h, jnp.pad(w_gate_b[j], ((0, 0), (0, LANES - ngate))).astype(bf16))
            q_p = _heads_prompt(qf, tp).astype(bf16)
            q_s = _heads_sample(qf, tp, ts, db).astype(bf16)

            tq = WINDOW
            iw = jnp.arange(tq)
            jj = jnp.arange(tq // BLK + 2)
            dist_c = iw[:, None] - (BLK * (jj[None, :] - 2) + BLK - 1)
            bc = _dist_bias(lut, dist_c, dist_c >= 0)
            bc = bc.transpose(0, 2, 1)[:, :, None, :]
            nqt = tp // tq
            qt_p = _heads_prompt(qf, tp).reshape(KV, G, nqt, tq, HEAD_DIM).transpose(0, 2, 4, 1, 3)
            qt_p = qt_p.reshape(KV, nqt, HEAD_DIM, G * tq).astype(bf16)
            oc_t, sel_t = _cmp_prompt(qt_p, kc_p, vc_p.transpose(0, 2, 1), bc,
                                      _head_rows(c31, tq).transpose(0, 2, 1), G, tq)
            oc_p = oc_t.reshape(KV, nqt, HEAD_DIM, G, tq).transpose(0, 3, 1, 4, 2).reshape(KV, G, tp, HEAD_DIM)
            sel_p = sel_t.transpose(0, 2, 1)
            dist_cs = (past + jnp.arange(ts))[:, None] - (BLK * jnp.arange(nbp)[None, :] + BLK - 1)
            bias_cs = _dist_bias(lut, dist_cs, dist_cs >= 0)
            cur_s = ((past + jnp.arange(ts)) // BLK).astype(i32).reshape(ts, 1)
            oc_s, sel_s = _cmp_sample(q_s, kc_s, vc_s, bias_cs, cur_s)

            tqs = SLC_TQ
            bias_sl = _toeplitz_bias(lut, tqs, 2 * tqs, tqs, sub=c31)
            kst_p = _kv_prompt(kvr[:, 2 * nk:3 * nk], tp).transpose(0, 2, 1).astype(bf16)
            vs_p = _kv_prompt(kvr[:, 3 * nk:4 * nk], tp)
            vs1_p = jnp.concatenate([vs_p, jnp.ones((KV, tp, 1), f32),
                                     jnp.zeros((KV, tp, LANES - HEAD_DIM - 1), f32)], axis=2).astype(bf16)
            os_p = _slc_prompt((_heads_prompt(qf, tp) * scale).astype(bf16), kst_p, vs1_p, sel_p,
                               _block_expand(sel_p.shape[-1], tp), bias_sl, tqs)

            rows = KV * G * ts
            eye = jnp.eye(KV, dtype=f32)
            qs_scaled = _heads_sample(qf, tp, ts, db) * scale
            qbd = (qs_scaled[:, :, :, None, :] * eye[None, :, None, :, None]).reshape(db, rows, nk).astype(bf16)
            selr = jnp.broadcast_to(sel_s[:, :, None], (db, KV, G, ts, nbp)).reshape(db, rows, nbp)
            ppt = min(SLC_PAGES_PER_TILE, npg)
            tk = ppt * psz
            bias_l = _toeplitz_bias(lut, ts, tk, tk, sub=c31).reshape(rows, tk)
            bias_n = _toeplitz_bias(lut, ts, BLK, 0, sub=c31).reshape(rows, BLK)
            new_t = jnp.pad(kv_s[:, :, 2 * nk:4 * nk], ((0, 0), (0, BLK - ts), (0, 0))).transpose(0, 2, 1)
            knew, vnew = new_t[:, :nk].astype(bf16), new_t[:, nk:].astype(bf16)
            os_s = _slc_sample(pool.reshape(pool.shape[0], 4, nk, psz), 2, page_table, qbd, selr,
                               _block_expand(nbp, past), knew, vnew, bias_l, bias_n,
                               _block_expand(nbp, BLK, past // BLK), ppt)
            os_s = os_s.reshape(db, KV, G * ts, HEAD_DIM)

            kw_p = _kv_prompt(kvr[:, 4 * nk:5 * nk], tp).astype(bf16)
            vw_p = _kv_prompt(kvr[:, 5 * nk:6 * nk], tp).astype(bf16)
            ow_p = _band_attn(q_p, kw_p, vw_p, bias_band)
            kw_s = kv_s[:, :, 4 * nk:5 * nk].reshape(db, ts, KV, HEAD_DIM)
            vw_s = kv_s[:, :, 5 * nk:6 * nk].reshape(db, ts, KV, HEAD_DIM)
            kk = jnp.concatenate([state_nsa_win[:, :, 0], kw_s], axis=1).transpose(0, 2, 1, 3)
            vv = jnp.concatenate([state_nsa_win[:, :, 1], vw_s], axis=1).transpose(0, 2, 1, 3)
            ow_s = _step_attn(q_s, kk.astype(bf16), vv.astype(bf16), bias_step)

            head_of_lane = jnp.arange(nq) // HEAD_DIM
            expand = (jnp.arange(LANES)[None, :, None]
                      == (jnp.arange(3)[:, None, None] * N_HEADS + head_of_lane[None, None, :])).astype(bf16)
            merged = _merge(glog, _unheads(oc_p, oc_s, tp, ts, db), _unheads(os_p, os_s, tp, ts, db),
                            _unheads(ow_p, ow_s, tp, ts, db), expand)
            x = _mm_res(merged, w_o_b[j].astype(bf16), x, mods, 2 * layer, tp)

        x = _moe_layer(x, norm_g[layer, 1].reshape(1, d), mods, 2 * layer + 1, tp, layer,
                       router_w[layer], router_b[layer], w_up, b_up, w_down, b_down)

    y = _rmsnorm(x, final_norm_g.reshape(1, d))
    y_prompt = y[:tp].reshape(1, tp, d)
    y_sample = y[tp:].reshape(ts, db, d).transpose(1, 0, 2)
    return (y_prompt, y_sample, jnp.stack(swa_p), jnp.stack(swa_s), nsa_rows_p, nsa_rows_s, nsa_win_p, nsa_win_s)
```

```python
import functools
import math

import jax
import jax.numpy as jnp
from jax import lax
from jax.experimental import pallas as pl
from jax.experimental.pallas import tpu as pltpu

f32 = jnp.float32
bf16 = jnp.bfloat16
i32 = jnp.int32

N_HEADS = 32
HEAD_DIM = 64
KV = 4
G = N_HEADS // KV
WINDOW = 128
BLK = 64
N_SEL = 16
TOP_K = 4
SWIGLU_LIMIT = 7.0
SWIGLU_ALPHA = 1.702
REL_BUCKETS = 32
REL_MAX_DIST = 128
EPS = 1e-6
NEG = -1e30
TINY = 1e-20
FORCE_SCORE = 1e4
LUT_N = 256
LANES = 128
VMEM_LIMIT = 56 * 1024 * 1024

TOKEN_TILE = 256
SLC_TQ = 256
SLC_ROWS = 128
SLC_PAGES_PER_TILE = 16
MOE_TM = 1280
MOE_SUB = 128
MOE_FC = 256
MOE_TT = 64


def _cparams(sem, vmem=None):
    return pltpu.CompilerParams(dimension_semantics=sem, vmem_limit_bytes=vmem)


def _col_tile(n, cap=512):
    tn = cap
    while n % tn:
        tn //= 2
    assert tn % LANES == 0
    return tn


def _dot(a, b):
    return jnp.dot(a, b, preferred_element_type=f32)


def _dot_nt(a, b):
    return lax.dot_general(a, b, (((1,), (1,)), ((), ())), preferred_element_type=f32)


def _rel_bucket(dist):
    n = jnp.maximum(dist, 0)
    max_exact = REL_BUCKETS // 2
    nf = jnp.maximum(n, 1).astype(f32)
    large = max_exact + (jnp.log(nf / max_exact) / math.log(REL_MAX_DIST / max_exact)
                         * (REL_BUCKETS - max_exact)).astype(i32)
    return jnp.where(n < max_exact, n, jnp.minimum(large, REL_BUCKETS - 1))


def _dist_bias(lut, dist, ok):
    b = lut[jnp.clip(dist, 0, LUT_N - 1)]
    b = jnp.where(ok[..., None], b, NEG)
    r, c = dist.shape
    return b.reshape(r, c, KV, G).transpose(2, 3, 0, 1).reshape(KV, G * r, c)


def _toeplitz_bias(lut, nrows, ncols, c, window=None, sub=None):
    lw = nrows + ncols
    dist = c + nrows - 1 - jnp.arange(lw)
    ok = dist >= 0
    if window is not None:
        ok = ok & (dist < window)
    vals = lut[jnp.clip(dist, 0, LUT_N - 1)]
    if sub is not None:
        vals = vals - sub[None, :]
    w = jnp.where(ok[:, None], vals, NEG).T
    skew = jnp.tile(w, (1, nrows))[:, :nrows * (lw - 1)].reshape(-1, nrows, lw - 1)
    return skew[:, :, nrows - 1:nrows - 1 + ncols].reshape(KV, G * nrows, ncols)


def _head_rows(vec, r):
    return jnp.broadcast_to(vec.reshape(KV, G, 1, 1), (KV, G, r, 1)).reshape(KV, G * r, 1).astype(f32)


def _ada_kernel(c_ref, w_ref, b_ref, o_ref):
    a = jax.nn.silu(c_ref[...]).astype(bf16)
    o_ref[...] = _dot(a, w_ref[...].astype(bf16)) + b_ref[...]


def _ada(cmat, w, b):
    nm, d, n = w.shape
    r = cmat.shape[0]
    tn = _col_tile(n)
    return pl.pallas_call(
        _ada_kernel,
        out_shape=jax.ShapeDtypeStruct((nm, r, n), f32),
        grid=(nm, n // tn),
        in_specs=[pl.BlockSpec((r, d), lambda m, j: (0, 0)),
                  pl.BlockSpec((None, d, tn), lambda m, j: (m, 0, j)),
                  pl.BlockSpec((None, 1, tn), lambda m, j: (m, 0, j))],
        out_specs=pl.BlockSpec((None, r, tn), lambda m, j: (m, 0, j)),
        compiler_params=_cparams(("arbitrary", "arbitrary")),
        name="ada_mods",
    )(cmat, w, b)


def _modulated(x_ref, g_ref, sh_ref, sc_ref):
    x = x_ref[...]
    tm, d = x.shape
    rg = sh_ref.shape[0]
    y = x * lax.rsqrt(jnp.mean(x * x, axis=-1, keepdims=True) + EPS) * g_ref[...]
    y = y.reshape(tm // rg, rg, d) * (1.0 + sc_ref[...])[None] + sh_ref[...][None]
    return y.reshape(tm, d)


def _modnorm_kernel(x_ref, g_ref, sh_ref, sc_ref, h_ref):
    h_ref[...] = _modulated(x_ref, g_ref, sh_ref, sc_ref).astype(bf16)


def _router_kernel(x_ref, g_ref, sh_ref, sc_ref, whi_ref, wlo_ref, rb_ref, h_ref, ti_ref, tw_ref):
    h = _modulated(x_ref, g_ref, sh_ref, sc_ref)
    hi = h.astype(bf16)
    lo = (h - hi.astype(f32)).astype(bf16)
    half = h.shape[1] // 2
    w_lo = lax.bitcast_convert_type(hi[:, :half].astype(f32), jnp.uint32)
    w_hi = lax.bitcast_convert_type(hi[:, half:].astype(f32), jnp.uint32)
    h_ref[...] = w_hi | lax.shift_right_logical(w_lo, jnp.uint32(16))
    logits = _dot(hi, whi_ref[...]) + (_dot(lo, whi_ref[...]) + _dot(hi, wlo_ref[...])) + rb_ref[...]
    lane = lax.broadcasted_iota(i32, logits.shape, 1).astype(f32)
    ti = jnp.zeros(logits.shape, f32)
    tv = jnp.zeros(logits.shape, f32)
    v0 = None
    den = None
    for k in range(TOP_K):
        m = jnp.max(logits, axis=-1, keepdims=True)
        idx = jnp.min(jnp.where(logits == m, lane, float(LANES)), axis=-1, keepdims=True)
        if k == 0:
            v0 = m
            e = jnp.ones_like(m)
            den = e
        else:
            e = jnp.exp(m - v0)
            den = den + e
        ti = jnp.where(lane == k, idx, ti)
        tv = jnp.where(lane == k, e, tv)
        logits = jnp.where(lane == idx, -jnp.inf, logits)
    ti_ref[...] = ti.astype(i32)
    tw_ref[...] = tv / den


def _mod_specs(d, rg, n_prompt_tiles, m_idx, cols):
    def spec(col):
        return pl.BlockSpec((None, rg, d), lambda i: (m_idx, jnp.where(i < n_prompt_tiles, 0, 1), col))
    return [spec(c) for c in cols]


def _modnorm(x, g, mods, m_idx, tp, router=None, tm=TOKEN_TILE):
    t, d = x.shape
    rg = mods.shape[1] // 2
    grid = (t // tm,)
    in_specs = [pl.BlockSpec((tm, d), lambda i: (i, 0)),
                pl.BlockSpec((1, d), lambda i: (0, 0))] + _mod_specs(d, rg, tp // tm, m_idx, (0, 1))
    h_spec = pl.BlockSpec((tm, d), lambda i: (i, 0))
    if router is None:
        return pl.pallas_call(
            _modnorm_kernel, out_shape=jax.ShapeDtypeStruct((t, d), bf16), grid=grid, in_specs=in_specs,
            out_specs=h_spec, compiler_params=_cparams(("arbitrary",)), name="modnorm",
        )(x, g, mods, mods)
    whi, wlo, rb = router
    in_specs += [pl.BlockSpec((d, LANES), lambda i: (0, 0)),
                 pl.BlockSpec((d, LANES), lambda i: (0, 0)),
                 pl.BlockSpec((1, LANES), lambda i: (0, 0))]
    lane_spec = pl.BlockSpec((tm, LANES), lambda i: (i, 0))
    return pl.pallas_call(
        _router_kernel,
        out_shape=(jax.ShapeDtypeStruct((t, d // 2), jnp.uint32), jax.ShapeDtypeStruct((t, LANES), i32),
                   jax.ShapeDtypeStruct((t, LANES), f32)),
        grid=grid, in_specs=in_specs,
        out_specs=(pl.BlockSpec((tm, d // 2), lambda i: (i, 0)), lane_spec, lane_spec),
        compiler_params=_cparams(("arbitrary",)), name="modnorm_router",
    )(x, g, mods, mods, whi, wlo, rb)


def _mm_kernel(a_ref, w_ref, o_ref):
    o_ref[...] = _dot(a_ref[...].astype(bf16), w_ref[...])


def _mm_res_kernel(a_ref, w_ref, x_ref, gt_ref, o_ref):
    acc = _dot(a_ref[...].astype(bf16), w_ref[...])
    tm, tn = acc.shape
    rg = gt_ref.shape[0]
    upd = acc.reshape(tm // rg, rg, tn) * gt_ref[...][None]
    o_ref[...] = x_ref[...] + upd.reshape(tm, tn)


def _mm(a, w, tm=TOKEN_TILE, tn=512):
    t, k = a.shape
    n = w.shape[1]
    tn = _col_tile(n, tn)
    return pl.pallas_call(
        _mm_kernel,
        out_shape=jax.ShapeDtypeStruct((t, n), f32),
        grid=(n // tn, t // tm),
        in_specs=[pl.BlockSpec((tm, k), lambda j, i: (i, 0)),
                  pl.BlockSpec((k, tn), lambda j, i: (0, j))],
        out_specs=pl.BlockSpec((tm, tn), lambda j, i: (i, j)),
        compiler_params=_cparams(("arbitrary", "arbitrary")),
        name="matmul",
    )(a, w)


def _mm_res(a, w, x, mods, m_idx, tp, tm=TOKEN_TILE, tn=512):
    t, k = a.shape
    n = w.shape[1]
    tn = _col_tile(n, tn)
    rg = mods.shape[1] // 2
    npt = tp // tm
    nj = n // tn
    return pl.pallas_call(
        _mm_res_kernel,
        out_shape=jax.ShapeDtypeStruct((t, n), f32),
        grid=(nj, t // tm),
        in_specs=[pl.BlockSpec((tm, k), lambda j, i: (i, 0)),
                  pl.BlockSpec((k, tn), lambda j, i: (0, j)),
                  pl.BlockSpec((tm, tn), lambda j, i: (i, j)),
                  pl.BlockSpec((None, rg, tn), lambda j, i: (m_idx, jnp.where(i < npt, 0, 1), 2 * nj + j))],
        out_specs=pl.BlockSpec((tm, tn), lambda j, i: (i, j)),
        compiler_params=_cparams(("arbitrary", "arbitrary")),
        name="matmul_residual",
    )(a, w, x, mods)


def _band_kernel(q_ref, kp_ref, kc_ref, vp_ref, vc_ref, b_ref, o_ref):
    g, tq, hd = q_ref.shape
    sink_keys = b_ref.shape[1] - 2 * tq
    has_sink = sink_keys > 0
    q = q_ref[...].reshape(g * tq, hd)
    ks = [kp_ref[...], kc_ref[...]]
    vs = [vp_ref[...], vc_ref[...]]
    if has_sink:
        ks.append(jnp.zeros((sink_keys, hd), bf16))
        vs.append(jnp.zeros((sink_keys, hd), bf16))
    kk = jnp.concatenate(ks, axis=0)
    vv = jnp.concatenate(vs, axis=0)
    s = _dot_nt(q, kk) * (hd ** -0.5) + b_ref[...]
    col = lax.broadcasted_iota(i32, s.shape, 1)
    first = pl.program_id(1) == 0
    s = jnp.where(jnp.logical_and(first, col < tq), NEG, s)
    m = jnp.max(s, axis=-1, keepdims=True)
    e = jnp.exp(s - m)
    den = jnp.sum(e, axis=-1, keepdims=True)
    o = _dot(e.astype(bf16), vv) / den
    o_ref[...] = o.reshape(g, tq, hd)


def _band_attn(q, k, v, bias, sink=None):
    _, g, t, hd = q.shape
    w = WINDOW
    if sink is not None:
        sink_cols = jnp.concatenate([_head_rows(sink, w), jnp.full((KV, g * w, LANES - 1), NEG, f32)], axis=2)
        bias = jnp.concatenate([bias, sink_cols], axis=2)
    nkeys = bias.shape[2]
    kspec_p = pl.BlockSpec((None, w, hd), lambda h, i: (h, jnp.maximum(i - 1, 0), 0))
    kspec_c = pl.BlockSpec((None, w, hd), lambda h, i: (h, i, 0))
    return pl.pallas_call(
        _band_kernel,
        out_shape=jax.ShapeDtypeStruct((KV, g, t, hd), f32),
        grid=(KV, t // w),
        in_specs=[pl.BlockSpec((None, g, w, hd), lambda h, i: (h, 0, i, 0)),
                  kspec_p, kspec_c, kspec_p, kspec_c,
                  pl.BlockSpec((None, g * w, nkeys), lambda h, i: (h, 0, 0))],
        out_specs=pl.BlockSpec((None, g, w, hd), lambda h, i: (h, 0, i, 0)),
        compiler_params=_cparams(("arbitrary", "arbitrary")),
        name="band_attn",
    )(q, k, k, v, v, bias)


def _step_kernel(q_ref, k_ref, v_ref, b_ref, *rest, has_sink):
    if has_sink:
        s_ref, o_ref = rest
    else:
        (o_ref,) = rest
    hd = q_ref.shape[-1]
    for h in range(KV):
        s = _dot_nt(q_ref[h], k_ref[h]) * (hd ** -0.5) + b_ref[h]
        m = jnp.max(s, axis=-1, keepdims=True)
        if has_sink:
            m = jnp.maximum(m, s_ref[h])
        e = jnp.exp(s - m)
        den = jnp.sum(e, axis=-1, keepdims=True)
        if has_sink:
            den = den + jnp.exp(s_ref[h] - m)
        o_ref[h] = _dot(e.astype(bf16), v_ref[h]) / den


def _step_attn(q, k, v, bias, sink=None):
    db, _, r, hd = q.shape
    l = k.shape[2]
    in_specs = [pl.BlockSpec((None, KV, r, hd), lambda b: (b, 0, 0, 0)),
                pl.BlockSpec((None, KV, l, hd), lambda b: (b, 0, 0, 0)),
                pl.BlockSpec((None, KV, l, hd), lambda b: (b, 0, 0, 0)),
                pl.BlockSpec((KV, r, l), lambda b: (0, 0, 0))]
    args = [q, k, v, bias]
    if sink is not None:
        in_specs.append(pl.BlockSpec((KV, r, 1), lambda b: (0, 0, 0)))
        args.append(sink)
    return pl.pallas_call(
        functools.partial(_step_kernel, has_sink=sink is not None),
        out_shape=jax.ShapeDtypeStruct((db, KV, r, hd), f32),
        grid=(db,),
        in_specs=in_specs,
        out_specs=pl.BlockSpec((None, KV, r, hd), lambda b: (b, 0, 0, 0)),
        compiler_params=_cparams(("arbitrary",)),
        name="step_attn",
    )(*args)


def _compress_kernel(pt_ref, pool_ref, pe_ref, w1_hbm, w2_ref, o_ref, buf, sem, w1_s, wsem, acc,
                     *, ch0, pgs, nchunk):
    b = pl.program_id(0)
    c = pl.program_id(1)
    step = b * nchunk + c
    nsteps = pl.num_programs(0) * nchunk
    slot = step % 2
    hd = buf.shape[3]
    nhid = w2_ref.shape[1]

    def page_copy(bb, cc, p, sl):
        page = pt_ref[bb, cc * pgs + p]
        return pltpu.make_async_copy(pool_ref.at[page, pl.ds(ch0, 2)], buf.at[sl, :, :, :, p, :], sem.at[sl])

    def start(st, sl):
        bb = st // nchunk
        cc = st % nchunk

        def body(p, carry):
            page_copy(bb, cc, p, sl).start()
            return carry
        lax.fori_loop(0, pgs, body, 0)

    @pl.when(step == 0)
    def _():
        w1_copy = pltpu.make_async_copy(w1_hbm, w1_s, wsem.at[0])
        w1_copy.start()
        start(step, slot)
        w1_copy.wait()

    @pl.when(step + 1 < nsteps)
    def _():
        start(step + 1, 1 - slot)

    def wait_body(p, carry):
        page_copy(b, c, p, slot).wait()
        return carry
    lax.fori_loop(0, pgs, wait_body, 0)

    for ch in range(2):
        for dp in range(hd // 2):
            parts = []
            for h in range(KV):
                pair = [(buf[slot, ch, h, 2 * dp + dd] + pe_ref[ch, 2 * dp + dd]).astype(bf16) for dd in range(2)]
                parts.append(jnp.concatenate(pair, axis=1))
            part = _dot(jnp.concatenate(parts, axis=0), w1_s[ch, dp])
            if dp == 0:
                acc[...] = part
            else:
                acc[...] += part
        hid = jax.nn.gelu(acc[...])
        for n in range(2):
            out = _dot(hid[:, n * nhid:(n + 1) * nhid].astype(bf16), w2_ref[ch])
            o_ref[ch, n] = out.reshape(KV, pgs, hd)


def _compress(pool, ch0, pt, pe_t, w1p, w2):
    nseq, npg = pt.shape
    psz = pool.shape[-1]
    pgs = min(npg, 64)
    nchunk = npg // pgs
    nhid = w2.shape[1]
    kern = functools.partial(_compress_kernel, ch0=ch0, pgs=pgs, nchunk=nchunk)
    out = pl.pallas_call(
        kern,
        out_shape=jax.ShapeDtypeStruct((nseq, 2, 2, KV, npg, HEAD_DIM), f32),
        grid_spec=pltpu.PrefetchScalarGridSpec(
            num_scalar_prefetch=1,
            grid=(nseq, nchunk),
            in_specs=[pl.BlockSpec(memory_space=pl.ANY),
                      pl.BlockSpec(pe_t.shape, lambda b, c, pt: (0, 0, 0, 0)),
                      pl.BlockSpec(memory_space=pl.ANY),
                      pl.BlockSpec(w2.shape, lambda b, c, pt: (0, 0, 0))],
            out_specs=pl.BlockSpec((None, 2, 2, KV, pgs, HEAD_DIM), lambda b, c, pt: (b, 0, 0, 0, c, 0)),
            scratch_shapes=[pltpu.VMEM((2, 2, KV, HEAD_DIM, pgs, psz), f32),
                            pltpu.SemaphoreType.DMA((2,)),
                            pltpu.VMEM(w1p.shape, bf16),
                            pltpu.SemaphoreType.DMA((1,)),
                            pltpu.VMEM((KV * pgs, 2 * nhid), f32)]),
        compiler_params=_cparams(("arbitrary", "arbitrary"), VMEM_LIMIT),
        name="compress",
    )(pt, pool, pe_t, w1p, w2)
    return out.transpose(0, 1, 3, 4, 2, 5).reshape(nseq, 2, KV, 2 * npg, HEAD_DIM)


def _cmp_attend(q, kc, vc, bias, g, tq):
    hd = q.shape[-1]
    nb = kc.shape[0]
    vis = bias > 0.5 * NEG
    s = _dot_nt(q, kc) * (hd ** -0.5) + bias
    m = jnp.max(s, axis=-1, keepdims=True)
    e = jnp.where(vis, jnp.exp(s - m), 0.0)
    p = e / jnp.maximum(jnp.sum(e, axis=-1, keepdims=True), TINY)
    return _dot(p.astype(bf16), vc), jnp.sum(p.reshape(g, tq, nb), axis=0)


def _select_blocks(imp, cur):
    r, nb = imp.shape
    col = lax.broadcasted_iota(i32, (r, nb), 1)
    colf = col.astype(f32)
    forced = (col == 0) | (col == cur) | (col == cur - 1)
    imp = jnp.where(forced, FORCE_SCORE, imp)
    imp = jnp.where(col <= cur, imp, NEG)
    sel = jnp.zeros((r, nb), f32)
    for _ in range(N_SEL):
        mx = jnp.max(imp, axis=-1, keepdims=True)
        idx = jnp.min(jnp.where(imp == mx, colf, float(nb)), axis=-1, keepdims=True)
        hit = colf == idx
        sel = jnp.where(hit, 1.0, sel)
        imp = jnp.where(hit, -jnp.inf, imp)
    return sel


def _cmp_prompt_kernel(qt_ref, kc_ref, vct_ref, bc_ref, c31_ref, o_ref, sel_ref, *, g, tq):
    hd, cols = qt_ref.shape
    nb = kc_ref.shape[0]
    qi = pl.program_id(1)
    blk = lax.broadcasted_iota(i32, (nb, cols), 0)
    first = (tq // BLK) * qi - 2
    bias = jnp.where(blk < first, c31_ref[...], NEG)
    for jj in range(bc_ref.shape[0]):
        bias = jnp.where(blk == first + jj, bc_ref[jj], bias)
    vis = bias > 0.5 * NEG
    s = _dot(kc_ref[...], qt_ref[...]) * (hd ** -0.5) + bias
    m = jnp.max(s, axis=0, keepdims=True)
    e = jnp.where(vis, jnp.exp(s - m), 0.0)
    p = e / jnp.maximum(jnp.sum(e, axis=0, keepdims=True), TINY)
    o_ref[...] = _dot(vct_ref[...], p.astype(bf16))

    imp = p[:, :tq]
    for gi in range(1, g):
        imp = imp + p[:, gi * tq:(gi + 1) * tq]
    blk = lax.broadcasted_iota(i32, (nb, tq), 0)
    blkf = blk.astype(f32)
    cur = (qi * tq + lax.broadcasted_iota(i32, (1, tq), 1)) // BLK
    forced = (blk == 0) | (blk == cur) | (blk == cur - 1)
    imp = jnp.where(forced, FORCE_SCORE, imp)
    imp = jnp.where(blk <= cur, imp, NEG)
    sel = jnp.zeros((nb, tq), f32)
    for _ in range(N_SEL):
        mx = jnp.max(imp, axis=0, keepdims=True)
        idx = jnp.min(jnp.where(imp == mx, blkf, float(nb)), axis=0, keepdims=True)
        hit = blkf == idx
        sel = jnp.where(hit, 1.0, sel)
        imp = jnp.where(hit, -jnp.inf, imp)
    sel_ref[...] = sel.astype(bf16)


def _cmp_prompt(qt, kc, vct, bc, c31, g, tq):
    _, nqt, hd, cols = qt.shape
    nb = kc.shape[1]
    nband = bc.shape[1]
    return pl.pallas_call(
        functools.partial(_cmp_prompt_kernel, g=g, tq=tq),
        out_shape=(jax.ShapeDtypeStruct((KV, nqt, hd, cols), f32),
                   jax.ShapeDtypeStruct((KV, nb, nqt * tq), bf16)),
        grid=(KV, nqt),
        in_specs=[pl.BlockSpec((None, None, hd, cols), lambda h, i: (h, i, 0, 0)),
                  pl.BlockSpec((None, nb, hd), lambda h, i: (h, 0, 0)),
                  pl.BlockSpec((None, hd, nb), lambda h, i: (h, 0, 0)),
                  pl.BlockSpec((None, nband, 1, cols), lambda h, i: (h, 0, 0, 0)),
                  pl.BlockSpec((None, 1, cols), lambda h, i: (h, 0, 0))],
        out_specs=(pl.BlockSpec((None, None, hd, cols), lambda h, i: (h, i, 0, 0)),
                   pl.BlockSpec((None, nb, tq), lambda h, i: (h, 0, i))),
        compiler_params=_cparams(("arbitrary", "arbitrary")),
        name="cmp_prompt",
    )(qt, kc, vct, bc, c31)


def _cmp_sample_kernel(q_ref, kc_ref, vc_ref, b_ref, cur_ref, o_ref, sel_ref):
    nseq = q_ref.shape[0]
    ts = cur_ref.shape[0]
    g = q_ref.shape[2] // ts
    imps = []
    for b in range(nseq):
        for h in range(KV):
            o, imp = _cmp_attend(q_ref[b, h], kc_ref[b, h], vc_ref[b, h], b_ref[h], g, ts)
            o_ref[b, h] = o
            imps.append(imp)
    sel = _select_blocks(jnp.concatenate(imps, axis=0), jnp.concatenate([cur_ref[...]] * (nseq * KV), axis=0))
    for b in range(nseq):
        for h in range(KV):
            i = b * KV + h
            sel_ref[b, h] = sel[i * ts:(i + 1) * ts].astype(bf16)


def _cmp_sample(q, kc, vc, bias, cur, nseq=4):
    db, _, r, hd = q.shape
    nbp = kc.shape[2]
    ts = cur.shape[0]
    return pl.pallas_call(
        _cmp_sample_kernel,
        out_shape=(jax.ShapeDtypeStruct((db, KV, r, hd), f32), jax.ShapeDtypeStruct((db, KV, ts, nbp), bf16)),
        grid=(db // nseq,),
        in_specs=[pl.BlockSpec((nseq, KV, r, hd), lambda b: (b, 0, 0, 0)),
                  pl.BlockSpec((nseq, KV, nbp, hd), lambda b: (b, 0, 0, 0)),
                  pl.BlockSpec((nseq, KV, nbp, hd), lambda b: (b, 0, 0, 0)),
                  pl.BlockSpec((KV, r, nbp), lambda b: (0, 0, 0)),
                  pl.BlockSpec((ts, 1), lambda b: (0, 0))],
        out_specs=(pl.BlockSpec((nseq, KV, r, hd), lambda b: (b, 0, 0, 0)),
                   pl.BlockSpec((nseq, KV, ts, nbp), lambda b: (b, 0, 0, 0))),
        compiler_params=_cparams(("arbitrary",)),
        name="cmp_sample",
    )(q, kc, vc, bias, cur)


def _softmax_chunk(a, vt_dot, m_ref, l_ref, acc_ref, rows):
    m_old = m_ref[rows, :]
    m_new = jnp.maximum(m_old, jnp.max(a, axis=-1, keepdims=True))
    alpha = jnp.exp(m_old - m_new)
    e = jnp.exp(a - m_new)
    l_ref[rows, :] = alpha * l_ref[rows, :] + jnp.sum(e, axis=-1, keepdims=True)
    acc_ref[rows, :] = alpha * acc_ref[rows, :] + vt_dot(e.astype(bf16))
    m_ref[rows, :] = m_new


def _slc_prompt_kernel(q_ref, kt_ref, v_ref, sel_ref, ex_ref, b_ref, o_ref,
                       s_scr, p_scr, madd_scr, m_scr, mt_scr, al_scr, acc_scr):
    g, tq, hd = q_ref.shape
    qi = pl.program_id(1)
    rows = g * tq
    rc = SLC_ROWS
    nchunk = rows // rc
    per_head = tq // rc

    def tile(kstart, width, bias_col):
        kt = kt_ref[:, pl.ds(kstart, width)]
        vt = v_ref[pl.ds(kstart, width), :]
        s_scr[:, :width] = _dot(q_ref[...].reshape(rows, hd), kt)
        msk = _dot(sel_ref[...], ex_ref[:, pl.ds(kstart, width)])
        madd_scr[:, :width] = jnp.where(msk > 0.5, 0.0, NEG)

        def logits(ci):
            part = ci % per_head
            a = s_scr[ci * rc:(ci + 1) * rc, :width] + madd_scr[part * rc:(part + 1) * rc, :width]
            if bias_col is not None:
                a = a + b_ref[ci * rc:(ci + 1) * rc, bias_col:bias_col + width]
            return a

        for ci in range(nchunk):
            mt_scr[ci * rc:(ci + 1) * rc, :] = jnp.broadcast_to(
                jnp.max(logits(ci), axis=-1, keepdims=True), (rc, LANES))
        m_old = m_scr[...]
        m_new = jnp.maximum(m_old, mt_scr[...])
        al_scr[...] = jnp.exp(m_old - m_new)
        m_scr[...] = m_new
        for ci in range(nchunk):
            m_c = m_scr[ci * rc:(ci + 1) * rc, :]
            e = jnp.exp(logits(ci) - jnp.tile(m_c, (1, width // LANES)))
            p_scr[ci * rc:(ci + 1) * rc, :width] = e.astype(bf16)
        acc_scr[...] = al_scr[...] * acc_scr[...] + _dot(p_scr[:, :width], vt)

    m_scr[...] = jnp.full(m_scr.shape, NEG, f32)
    acc_scr[...] = jnp.zeros(acc_scr.shape, f32)

    @pl.when(qi == 0)
    def _():
        tile(0, tq, tq)

    @pl.when(qi > 0)
    def _():
        tile(pl.multiple_of((qi - 1) * tq, tq), 2 * tq, 0)

    nfar = jnp.maximum(qi - 1, 0)

    def far(j, carry):
        tile(pl.multiple_of(j * 2 * tq, 2 * tq), 2 * tq, None)
        return carry
    lax.fori_loop(0, nfar // 2, far, 0)

    @pl.when(nfar % 2 == 1)
    def _():
        tile(pl.multiple_of((nfar - 1) * tq, tq), tq, None)

    acc = acc_scr[...]
    o_ref[...] = (acc[:, :hd] / acc[:, hd:hd + 1]).reshape(g, tq, hd)


def _slc_prompt(q, kt, v1, sel, expand, bias, tq):
    _, g, t, hd = q.shape
    nb = sel.shape[-1]
    rows = g * tq
    return pl.pallas_call(
        _slc_prompt_kernel,
        out_shape=jax.ShapeDtypeStruct((KV, g, t, hd), f32),
        grid=(KV, t // tq),
        in_specs=[pl.BlockSpec((None, g, tq, hd), lambda h, i: (h, 0, i, 0)),
                  pl.BlockSpec((None, hd, t), lambda h, i: (h, 0, 0)),
                  pl.BlockSpec((None, t, LANES), lambda h, i: (h, 0, 0)),
                  pl.BlockSpec((None, tq, nb), lambda h, i: (h, i, 0)),
                  pl.BlockSpec((nb, t), lambda h, i: (0, 0)),
                  pl.BlockSpec((None, rows, 2 * tq), lambda h, i: (h, 0, 0))],
        out_specs=pl.BlockSpec((None, g, tq, hd), lambda h, i: (h, 0, i, 0)),
        scratch_shapes=[pltpu.VMEM((rows, 2 * tq), f32), pltpu.VMEM((rows, 2 * tq), bf16),
                        pltpu.VMEM((tq, 2 * tq), f32), pltpu.VMEM((rows, LANES), f32),
                        pltpu.VMEM((rows, LANES), f32), pltpu.VMEM((rows, LANES), f32),
                        pltpu.VMEM((rows, LANES), f32)],
        compiler_params=_cparams(("arbitrary", "arbitrary"), VMEM_LIMIT),
        name="slc_prompt",
    )(q, kt, v1, sel, expand, bias)


def _slc_sample_kernel(pt_ref, pool_ref, q_ref, sel_ref, ex_ref, kn_ref, vn_ref, bl_ref, bn_ref, exn_ref, o_ref,
                       buf, sem, madd_ref, m_ref, l_ref, acc_ref, *, ch0, pgs, nchunk, ppt):
    b = pl.program_id(0)
    c = pl.program_id(1)
    step = b * nchunk + c
    nsteps = pl.num_programs(0) * nchunk
    slot = step % 2
    psz = buf.shape[-1]
    rows = q_ref.shape[0]
    ntile = pgs // ppt
    tk = ppt * psz
    last = c == nchunk - 1

    def page_copy(bb, cc, p, sl):
        page = pt_ref[bb, cc * pgs + p]
        return pltpu.make_async_copy(pool_ref.at[page, pl.ds(ch0, 2)], buf.at[sl, p], sem.at[sl])

    def start(st, sl):
        bb = st // nchunk
        cc = st % nchunk

        def body(p, carry):
            page_copy(bb, cc, p, sl).start()
            return carry
        lax.fori_loop(0, pgs, body, 0)

    @pl.when(step == 0)
    def _():
        start(step, slot)

    @pl.when(step + 1 < nsteps)
    def _():
        start(step + 1, 1 - slot)

    def wait_body(p, carry):
        page_copy(b, c, p, slot).wait()
        return carry
    lax.fori_loop(0, pgs, wait_body, 0)

    @pl.when(c == 0)
    def _():
        m_ref[...] = jnp.full(m_ref.shape, NEG, f32)
        l_ref[...] = jnp.zeros(l_ref.shape, f32)
        acc_ref[...] = jnp.zeros(acc_ref.shape, f32)

    madd_ref[...] = jnp.where(_dot(sel_ref[...], ex_ref[...]) > 0.5, 0.0, NEG)

    def tile(j, with_bias):
        kt = jnp.concatenate([buf[slot, j * ppt + p, 0] for p in range(ppt)], axis=1).astype(bf16)
        vt = jnp.concatenate([buf[slot, j * ppt + p, 1] for p in range(ppt)], axis=1).astype(bf16)
        col0 = pl.multiple_of(j * tk, tk)
        a = _dot(q_ref[...], kt) + madd_ref[:, pl.ds(col0, tk)]
        if with_bias:
            a = a + bl_ref[...]
        _softmax_chunk(a, lambda p: _dot_nt(p, vt), m_ref, l_ref, acc_ref, slice(None))

    def plain(j, carry):
        tile(j, False)
        return carry
    lax.fori_loop(0, ntile - jnp.where(last, 1, 0), plain, 0)

    @pl.when(last)
    def _():
        tile(ntile - 1, True)
        madd_new = jnp.where(_dot(sel_ref[...], exn_ref[...]) > 0.5, 0.0, NEG)
        vn = vn_ref[...]
        a = _dot(q_ref[...], kn_ref[...]) + madd_new + bn_ref[...]
        _softmax_chunk(a, lambda p: _dot_nt(p, vn), m_ref, l_ref, acc_ref, slice(None))
        full = acc_ref[...] / l_ref[...]
        r = rows // KV
        o_ref[...] = jnp.concatenate(
            [full[h * r:(h + 1) * r, h * HEAD_DIM:(h + 1) * HEAD_DIM] for h in range(KV)], axis=0)


def _slc_sample(pool, ch0, pt, qbd, selr, expand, knew, vnew, bias_last, bias_new, expand_new, ppt=4):
    db, npg = pt.shape
    psz = pool.shape[-1]
    rows = qbd.shape[1]
    nkd = KV * HEAD_DIM
    pgs = min(npg, 32)
    nchunk = npg // pgs
    ppt = min(ppt, pgs)
    nbp = selr.shape[-1]
    ck = pgs * psz
    kern = functools.partial(_slc_sample_kernel, ch0=ch0, pgs=pgs, nchunk=nchunk, ppt=ppt)
    return pl.pallas_call(
        kern,
        out_shape=jax.ShapeDtypeStruct((db, rows, HEAD_DIM), f32),
        grid_spec=pltpu.PrefetchScalarGridSpec(
            num_scalar_prefetch=1,
            grid=(db, nchunk),
            in_specs=[pl.BlockSpec(memory_space=pl.ANY),
                      pl.BlockSpec((None, rows, nkd), lambda b, c, pt: (b, 0, 0)),
                      pl.BlockSpec((None, rows, nbp), lambda b, c, pt: (b, 0, 0)),
                      pl.BlockSpec((nbp, ck), lambda b, c, pt: (0, c)),
                      pl.BlockSpec((None, nkd, BLK), lambda b, c, pt: (b, 0, 0)),
                      pl.BlockSpec((None, nkd, BLK), lambda b, c, pt: (b, 0, 0)),
                      pl.BlockSpec((rows, ppt * psz), lambda b, c, pt: (0, 0)),
                      pl.BlockSpec((rows, BLK), lambda b, c, pt: (0, 0)),
                      pl.BlockSpec((nbp, BLK), lambda b, c, pt: (0, 0))],
            out_specs=pl.BlockSpec((None, rows, HEAD_DIM), lambda b, c, pt: (b, 0, 0)),
            scratch_shapes=[pltpu.VMEM((2, pgs, 2, nkd, psz), f32),
                            pltpu.SemaphoreType.DMA((2,)),
                            pltpu.VMEM((rows, ck), f32),
                            pltpu.VMEM((rows, 1), f32), pltpu.VMEM((rows, 1), f32),
                            pltpu.VMEM((rows, nkd), f32)]),
        compiler_params=_cparams(("arbitrary", "arbitrary"), VMEM_LIMIT),
        name="slc_sample",
    )(pt, pool, qbd, selr, expand, knew, vnew, bias_last, bias_new, expand_new)


def _merge_kernel(gl_ref, oc_ref, os_ref, ow_ref, ex_ref, o_ref):
    gate = jax.nn.sigmoid(gl_ref[...])
    hi = gate.astype(bf16)
    lo = (gate - hi.astype(f32)).astype(bf16)
    acc = None
    for br, ref in enumerate((oc_ref, os_ref, ow_ref)):
        ex = ex_ref[br]
        gx = _dot(hi, ex) + _dot(lo, ex)
        term = gx * ref[...]
        acc = term if acc is None else acc + term
    o_ref[...] = acc.astype(bf16)


def _merge(glog, oc, osl, ow, expand, tm=TOKEN_TILE):
    t, n = oc.shape
    return pl.pallas_call(
        _merge_kernel,
        out_shape=jax.ShapeDtypeStruct((t, n), bf16),
        grid=(t // tm,),
        in_specs=[pl.BlockSpec((tm, LANES), lambda i: (i, 0)),
                  pl.BlockSpec((tm, n), lambda i: (i, 0)),
                  pl.BlockSpec((tm, n), lambda i: (i, 0)),
                  pl.BlockSpec((tm, n), lambda i: (i, 0)),
                  pl.BlockSpec((3, LANES, n), lambda i: (0, 0, 0))],
        out_specs=pl.BlockSpec((tm, n), lambda i: (i, 0)),
        compiler_params=_cparams(("arbitrary",)),
        name="nsa_merge",
    )(glog, oc, osl, ow, expand)


def _moe_kernel(te_ref, ns_ref, ic_ref, in_ref, h_ref, wg_ref, wl_ref, bg_ref, bl_ref, wd_ref, bd_ref, o_ref,
                xbuf, sem, xs, wg_s, wl_s, wd_s, *, sub):
    t = pl.program_id(0)
    f = pl.program_id(1)
    nt = pl.num_programs(0)
    nsub = ns_ref[t]

    half = xbuf.shape[1]
    unroll = 8

    def row_copy(idx_ref, r):
        return pltpu.make_async_copy(h_ref.at[pl.ds(idx_ref[0, r], 1)], xbuf.at[pl.ds(r, 1)], sem.at[0])

    def for_rows(nrows, fn):
        def body(i, carry):
            for u in range(unroll):
                fn(i * unroll + u)
            return carry
        lax.fori_loop(0, nrows // unroll, body, 0)

    def start(idx_ref, nrows):
        for_rows(nrows, lambda r: row_copy(idx_ref, r).start())

    @pl.when(f == 0)
    def _():
        o_ref[...] = jnp.zeros(o_ref.shape, f32)

        @pl.when(jnp.logical_and(t == 0, nsub > 0))
        def _():
            start(ic_ref, nsub * sub)

        @pl.when(nsub > 0)
        def _():
            for_rows(nsub * sub, lambda r: row_copy(ic_ref, r).wait())

            def unpack_body(s, carry):
                r = pl.multiple_of(s * sub, sub)
                w = xbuf[pl.ds(r, sub), :]
                lo = lax.bitcast_convert_type(lax.shift_left(w, jnp.uint32(16)), f32)
                hi = lax.bitcast_convert_type(w & jnp.uint32(0xFFFF0000), f32)
                xs[pl.ds(r, sub), :half] = lo.astype(bf16)
                xs[pl.ds(r, sub), half:] = hi.astype(bf16)
                return carry
            lax.fori_loop(0, nsub, unpack_body, 0)

        nnext = ns_ref[jnp.minimum(t + 1, nt - 1)]

        @pl.when(jnp.logical_and(t + 1 < nt, nnext > 0))
        def _():
            start(in_ref, nnext * sub)

    @pl.when(nsub > 0)
    def _():
        wg_s[...] = wg_ref[...].astype(bf16)
        wl_s[...] = wl_ref[...].astype(bf16)
        wd_s[...] = wd_ref[...].astype(bf16)
        first = jnp.where(f == 0, 1.0, 0.0)

        def chunk(r, rows):
            x = xs[pl.ds(r, rows), :]
            glu = jnp.minimum(_dot(x, wg_s[...]) + bg_ref[...], SWIGLU_LIMIT)
            lin = jnp.clip(_dot(x, wl_s[...]) + bl_ref[...], -SWIGLU_LIMIT, SWIGLU_LIMIT)
            act = glu * jax.nn.sigmoid(SWIGLU_ALPHA * glu) * (lin + 1.0)
            y = _dot(act.astype(bf16), wd_s[...])
            o_ref[pl.ds(r, rows), :] += y + first * bd_ref[...]

        def quad(j, carry):
            chunk(pl.multiple_of(j * 4 * sub, 4 * sub), 4 * sub)
            return carry
        nquad = nsub // 4
        lax.fori_loop(0, nquad, quad, 0)
        rem = nsub - 4 * nquad

        @pl.when(rem >= 2)
        def _():
            chunk(pl.multiple_of(nquad * 4 * sub, 2 * sub), 2 * sub)

        @pl.when(rem % 2 == 1)
        def _():
            chunk(pl.multiple_of((nsub - 1) * sub, sub), sub)


def _moe_experts(h, src_tok, tile_e, tile_ns, layer, w_up, b_up, w_down, b_down):
    t = h.shape[0]
    d = 2 * h.shape[1]
    nt, _, tm = src_tok.shape
    ff = w_down.shape[2]
    fc = min(MOE_FC, ff)
    nf = ff // fc

    def fidx(ti, f, ns):
        return jnp.where(ns[ti] > 0, f, 0)

    idx_c = pl.BlockSpec((None, 1, tm), lambda ti, f, te, ns: (ti, 0, 0), memory_space=pltpu.SMEM)
    idx_n = pl.BlockSpec((None, 1, tm), lambda ti, f, te, ns: (jnp.minimum(ti + 1, nt - 1), 0, 0),
                         memory_space=pltpu.SMEM)
    return pl.pallas_call(
        functools.partial(_moe_kernel, sub=MOE_SUB),
        out_shape=jax.ShapeDtypeStruct((nt * tm, d), f32),
        grid_spec=pltpu.PrefetchScalarGridSpec(
            num_scalar_prefetch=2,
            grid=(nt, nf),
            in_specs=[idx_c, idx_n,
                      pl.BlockSpec(memory_space=pl.ANY),
                      pl.BlockSpec((None, None, d, fc), lambda ti, f, te, ns: (layer, te[ti], 0, fidx(ti, f, ns))),
                      pl.BlockSpec((None, None, d, fc),
                                   lambda ti, f, te, ns: (layer, te[ti], 0, nf + fidx(ti, f, ns))),
                      pl.BlockSpec((None, None, 1, fc), lambda ti, f, te, ns: (layer, te[ti], 0, fidx(ti, f, ns))),
                      pl.BlockSpec((None, None, 1, fc),
                                   lambda ti, f, te, ns: (layer, te[ti], 0, nf + fidx(ti, f, ns))),
                      pl.BlockSpec((None, None, fc, d), lambda ti, f, te, ns: (layer, te[ti], fidx(ti, f, ns), 0)),
                      pl.BlockSpec((None, None, 1, d), lambda ti, f, te, ns: (layer, te[ti], 0, 0))],
            out_specs=pl.BlockSpec((tm, d), lambda ti, f, te, ns: (ti, 0)),
            scratch_shapes=[pltpu.VMEM((tm, d // 2), jnp.uint32), pltpu.SemaphoreType.DMA((1,)),
                            pltpu.VMEM((tm, d), bf16),
                            pltpu.VMEM((d, fc), bf16), pltpu.VMEM((d, fc), bf16), pltpu.VMEM((fc, d), bf16)]),
        compiler_params=_cparams(("arbitrary", "arbitrary"), VMEM_LIMIT),
        name="moe_experts",
    )(tile_e, tile_ns, src_tok, src_tok, h, w_up, w_up, b_up, b_up, w_down, b_down)


def _combine_kernel(ic_ref, in_ref, y_ref, x_ref, tw_ref, gt_ref, o_ref, buf, sem):
    step = pl.program_id(0)
    nsteps = pl.num_programs(0)
    slot = step % 2
    nrow = buf.shape[1]

    unroll = 8

    def row_copy(idx_ref, r, sl):
        return pltpu.make_async_copy(y_ref.at[pl.ds(idx_ref[0, r], 1)], buf.at[sl, pl.ds(r, 1)], sem.at[sl])

    def for_rows(fn):
        def body(i, carry):
            for u in range(unroll):
                fn(i * unroll + u)
            return carry
        lax.fori_loop(0, nrow // unroll, body, 0)

    @pl.when(step == 0)
    def _():
        for_rows(lambda r: row_copy(ic_ref, r, slot).start())

    @pl.when(step + 1 < nsteps)
    def _():
        for_rows(lambda r: row_copy(in_ref, r, 1 - slot).start())

    for_rows(lambda r: row_copy(ic_ref, r, slot).wait())

    tt, d = x_ref.shape
    rg = gt_ref.shape[0]
    tw = tw_ref[...]
    acc = None
    for k in range(TOP_K):
        term = tw[:, k:k + 1] * buf[slot, pl.ds(k * tt, tt), :]
        acc = term if acc is None else acc + term
    if tt >= rg:
        upd = (acc.reshape(tt // rg, rg, d) * gt_ref[...][None]).reshape(tt, d)
    else:
        part = step % (rg // tt)
        upd = acc * gt_ref[pl.ds(pl.multiple_of(part * tt, tt), tt), :]
    o_ref[...] = x_ref[...] + upd


def _moe_combine(y, pos, x, topw, mods, m_idx, tp):
    t, d = x.shape
    tt = MOE_TT
    nsteps = t // tt
    rg = mods.shape[1] // 2
    npt = tp // tt
    n_idx = TOP_K * tt
    idx_c = pl.BlockSpec((None, 1, n_idx), lambda i: (i, 0, 0), memory_space=pltpu.SMEM)
    idx_n = pl.BlockSpec((None, 1, n_idx), lambda i: (jnp.minimum(i + 1, nsteps - 1), 0, 0),
                         memory_space=pltpu.SMEM)
    return pl.pallas_call(
        _combine_kernel,
        out_shape=jax.ShapeDtypeStruct((t, d), f32),
        grid=(nsteps,),
        in_specs=[idx_c, idx_n,
                  pl.BlockSpec(memory_space=pl.ANY),
                  pl.BlockSpec((tt, d), lambda i: (i, 0)),
                  pl.BlockSpec((tt, LANES), lambda i: (i, 0)),
                  pl.BlockSpec((None, rg, d), lambda i: (m_idx, jnp.where(i < npt, 0, 1), 2))],
        out_specs=pl.BlockSpec((tt, d), lambda i: (i, 0)),
        scratch_shapes=[pltpu.VMEM((2, n_idx, d), f32), pltpu.SemaphoreType.DMA((2,))],
        compiler_params=_cparams(("arbitrary",)),
        name="moe_combine",
    )(pos, pos, y, x, topw, mods)


def _moe_layer(x, g, mods, m_idx, tp, layer, router_w, router_b, w_up, b_up, w_down, b_down):
    t, d = x.shape
    ne = router_w.shape[1]
    ff = w_down.shape[2]
    tm, sub, tt = MOE_TM, MOE_SUB, MOE_TT

    rw = jnp.pad(router_w, ((0, 0), (0, LANES - ne)))
    whi = rw.astype(bf16)
    wlo = (rw - whi.astype(f32)).astype(bf16)
    rb = jnp.pad(router_b.reshape(1, ne), ((0, 0), (0, LANES - ne)), constant_values=-jnp.inf)
    h, topi, topw = _modnorm(x, g, mods, m_idx, tp, router=(whi, wlo, rb))

    e_flat = topi[:, :TOP_K].reshape(-1)
    npair = t * TOP_K
    onehot = (e_flat[:, None] == jnp.arange(ne, dtype=i32)[None, :]).astype(i32)
    counts = jnp.sum(onehot, axis=0)
    order = jnp.argsort(e_flat, stable=True).astype(i32)
    inv = jnp.argsort(order).astype(i32)
    tiles_e = (counts + tm - 1) // tm
    cum_tiles = jnp.cumsum(tiles_e)
    tile_start = cum_tiles - tiles_e
    grp_start = jnp.cumsum(counts) - counts
    nt = npair // tm + ne
    tile_ids = jnp.arange(nt, dtype=i32)
    tile_e = jnp.clip(jnp.searchsorted(cum_tiles, tile_ids, side="right"), 0, ne - 1).astype(i32)
    tile_off = (tile_ids - tile_start[tile_e]) * tm
    rows_in_tile = jnp.where(tile_ids < cum_tiles[-1], jnp.clip(counts[tile_e] - tile_off, 0, tm), 0)
    tile_ns = ((rows_in_tile + sub - 1) // sub).astype(i32)
    pad_e = tiles_e * tm - counts
    pad_key = jnp.where(jnp.arange(tm, dtype=i32)[None, :] < pad_e[:, None], jnp.arange(ne, dtype=i32)[:, None], ne)
    keys = jnp.concatenate([e_flat, pad_key.reshape(-1)])
    toks = jnp.concatenate([jnp.arange(npair, dtype=i32) // TOP_K, jnp.zeros((ne * tm,), i32)])
    src_tok = lax.sort_key_val(keys, toks, is_stable=True)[1][:nt * tm]
    pair_tile_start = jnp.sum(onehot * tile_start[None, :], axis=1)
    pair_grp_start = jnp.sum(onehot * grp_start[None, :], axis=1)
    pos = pair_tile_start * tm + (inv - pair_grp_start)

    y = _moe_experts(h, src_tok.reshape(nt, 1, tm).astype(i32), tile_e, tile_ns, layer,
                     w_up, b_up.reshape(b_up.shape[0], ne, 1, 2 * ff), w_down,
                     b_down.reshape(b_down.shape[0], ne, 1, d))
    pos_steps = pos.astype(i32).reshape(t // tt, tt, TOP_K).transpose(0, 2, 1).reshape(t // tt, 1, TOP_K * tt)
    return _moe_combine(y, pos_steps, x, topw, mods, m_idx, tp)


def _rms_kernel(x_ref, g_ref, o_ref):
    x = x_ref[...]
    o_ref[...] = x * lax.rsqrt(jnp.mean(x * x, axis=-1, keepdims=True) + EPS) * g_ref[...]


def _rmsnorm(x, g, tm=TOKEN_TILE):
    t, d = x.shape
    return pl.pallas_call(
        _rms_kernel,
        out_shape=jax.ShapeDtypeStruct((t, d), f32),
        grid=(t // tm,),
        in_specs=[pl.BlockSpec((tm, d), lambda i: (i, 0)), pl.BlockSpec((1, d), lambda i: (0, 0))],
        out_specs=pl.BlockSpec((tm, d), lambda i: (i, 0)),
        compiler_params=_cparams(("arbitrary",)),
        name="final_norm",
    )(x, g)


def _heads_prompt(a, tp):
    return a[:tp].reshape(tp, KV, G, HEAD_DIM).transpose(1, 2, 0, 3)


def _heads_sample(a, tp, ts, db):
    return a[tp:].reshape(ts, db, KV, G, HEAD_DIM).transpose(1, 2, 3, 0, 4).reshape(db, KV, G * ts, HEAD_DIM)


def _unheads(op, osm, tp, ts, db):
    a = op.transpose(2, 0, 1, 3).reshape(tp, N_HEADS * HEAD_DIM)
    b = osm.reshape(db, KV, G, ts, HEAD_DIM).transpose(3, 0, 1, 2, 4).reshape(ts * db, N_HEADS * HEAD_DIM)
    return jnp.concatenate([a, b], axis=0)


def _kv_prompt(a, tp):
    return a[:tp].reshape(tp, KV, HEAD_DIM).transpose(1, 0, 2)


def _kv_sample(a, tp, ts, db):
    return a[tp:].reshape(ts, db, KV, HEAD_DIM).transpose(1, 0, 2, 3)


def _pages_dim_major(rows, psz):
    n = rows.shape[0] // psz
    return rows.reshape(n, psz, -1, KV, HEAD_DIM).transpose(0, 2, 3, 4, 1)


def _block_expand(nb, nkeys, first_block=0):
    blk = first_block + jnp.arange(nkeys, dtype=i32) // BLK
    return (jnp.arange(nb, dtype=i32)[:, None] == blk[None, :]).astype(bf16)


def kernel(x_prompt, x_sample, state_swa, cache_nsa, state_nsa_win, page_table, c_prompt, c_sample,
           rel_table, w_qkv_a, sink_a, w_o_a, w_q_b, w_gate_b, w_o_b, w_kv_shared, cmp_pe, cmp_w1, cmp_w2,
           router_w, router_b, w_up, b_up, w_down, b_down, norm_g, ada_w, ada_b,
           kv_norm_g, ada_kv_w, ada_kv_b, final_norm_g):
    bsz, tp, d = x_prompt.shape
    db, ts, _ = x_sample.shape
    n_a = w_qkv_a.shape[0]
    depth = norm_g.shape[0]
    npg = page_table.shape[1]
    psz = cache_nsa.shape[1]
    past = npg * psz
    wb = state_swa.shape[2]
    nq = N_HEADS * HEAD_DIM
    nk = KV * HEAD_DIM
    scale = HEAD_DIM ** -0.5
    assert bsz == 1 and wb == WINDOW and state_nsa_win.shape[1] == WINDOW
    assert psz == 2 * BLK and past % BLK == 0 and ts <= BLK and tp % SLC_TQ == 0 and tp % TOKEN_TILE == 0
    t = tp + ts * db

    x = jnp.concatenate([x_prompt[0], x_sample.transpose(1, 0, 2).reshape(ts * db, d)], axis=0)
    cmat = jnp.concatenate([jnp.broadcast_to(c_prompt, (db, d)), c_sample], axis=0)
    mods = _ada(cmat, ada_w.reshape(depth * 2, d, 3 * d), ada_b.reshape(depth * 2, 1, 3 * d))
    mods_kv = _ada(cmat, ada_kv_w[None], ada_kv_b[None, None])

    lut = rel_table[_rel_bucket(jnp.arange(LUT_N))].astype(f32)
    c31 = lut[LUT_N - 1]
    bias_band = _toeplitz_bias(lut, WINDOW, 2 * WINDOW, WINDOW, window=WINDOW)
    bias_step = _toeplitz_bias(lut, ts, wb + ts, wb, window=WINDOW)

    swa_p, swa_s = [], []
    for layer in range(depth):
        if layer == n_a:
            h_kv = _modnorm(x, kv_norm_g.reshape(1, d), mods_kv, 0, tp)
            kvr = _mm(h_kv, w_kv_shared.astype(bf16))
            kv_s = kvr[tp:].reshape(ts, db, 6 * nk).transpose(1, 0, 2)
            nsa_rows_p = kvr[:tp, :4 * nk].reshape(1, tp, 4, KV, HEAD_DIM)
            nsa_rows_s = kv_s[:, :, :4 * nk].reshape(db, ts, 4, KV, HEAD_DIM)
            nsa_win_p = kvr[tp - WINDOW:tp, 4 * nk:].reshape(1, WINDOW, 2, KV, HEAD_DIM)
            nsa_win_s = jnp.concatenate(
                [state_nsa_win, kv_s[:, :, 4 * nk:].reshape(db, ts, 2, KV, HEAD_DIM)], axis=1)[:, -WINDOW:]

            nhid = cmp_w1.shape[-1]
            pe_t = jnp.tile(cmp_pe.transpose(0, 2, 1), (1, 1, 2)).reshape(2, HEAD_DIM, 1, psz)
            w1t = cmp_w1.reshape(2, BLK, HEAD_DIM // 2, 2, nhid).transpose(0, 2, 3, 1, 4)
            eye2 = jnp.eye(2, dtype=f32)
            w1p = (w1t[:, :, :, None, :, None, :] * eye2[None, None, None, :, None, :, None]).reshape(
                2, HEAD_DIM // 2, 2 * psz, 2 * nhid).astype(bf16)
            w2b = cmp_w2.astype(bf16)
            npg_p = tp // psz
            cmp_p = _compress(_pages_dim_major(kvr[:tp, :2 * nk], psz), 0,
                              jnp.arange(npg_p, dtype=i32)[None], pe_t, w1p, w2b)[0]
            pool = cache_nsa.transpose(0, 2, 3, 4, 1)
            cmp_s = _compress(pool, 0, page_table, pe_t, w1p, w2b)
            tail = jnp.pad(kv_s[:, :, :2 * nk], ((0, 0), (0, psz - ts), (0, 0))).reshape(db * psz, 2 * nk)
            cmp_t = _compress(_pages_dim_major(tail, psz), 0, jnp.arange(db, dtype=i32)[None], pe_t, w1p, w2b)[0]
            cmp_t = cmp_t.reshape(2, KV, db, 2, HEAD_DIM)[:, :, :, 0].transpose(2, 0, 1, 3)
            nb_s = past // BLK + 1
            nbp = ((nb_s + 255) // 256) * 256
            cmp_s = jnp.concatenate(
                [cmp_s, cmp_t[:, :, :, None], jnp.zeros((db, 2, KV, nbp - nb_s, HEAD_DIM), f32)], axis=3)
            kc_p, vc_p = cmp_p[0].astype(bf16), cmp_p[1].astype(bf16)
            kc_s, vc_s = cmp_s[:, 0].astype(bf16), cmp_s[:, 1].astype(bf16)

        g_mix = norm_g[layer, 0].reshape(1, d)
        h = _modnorm(x, g_mix, mods, 2 * layer, tp)
        if layer < n_a:
            y = _mm(h, w_qkv_a[layer].astype(bf16))
            q, k, v = y[:, :nq], y[:, nq:nq + nk], y[:, nq + nk:]
            sink = sink_a[layer].astype(f32)
            o_p = _band_attn(_heads_prompt(q, tp).astype(bf16), _kv_prompt(k, tp).astype(bf16),
                             _kv_prompt(v, tp).astype(bf16), bias_band, sink)
            k_s, v_s = _kv_sample(k, tp, ts, db), _kv_sample(v, tp, ts, db)
            kk = jnp.concatenate([state_swa[layer, :, :, 0], k_s], axis=1).transpose(0, 2, 1, 3)
            vv = jnp.concatenate([state_swa[layer, :, :, 1], v_s], axis=1).transpose(0, 2, 1, 3)
            o_s = _step_attn(_heads_sample(q, tp, ts, db).astype(bf16), kk.astype(bf16), vv.astype(bf16),
                             bias_step, _head_rows(sink, ts))
            swa_p.append(jnp.stack([k[tp - wb:tp].reshape(1, wb, KV, HEAD_DIM),
                                    v[tp - wb:tp].reshape(1, wb, KV, HEAD_DIM)], axis=2))
            swa_s.append(jnp.concatenate([state_swa[layer], jnp.stack([k_s, v_s], axis=2)], axis=1)[:, -wb:])
            x = _mm_res(_unheads(o_p, o_s, tp, ts, db), w_o_a[layer].astype(bf16), x, mods, 2 * layer, tp)
        else:
            j = layer - n_a
            qf = _mm(h, w_q_b[j].astype(bf16))
            ngate = 3 * N_HEADS
            glog = _mm(h, jnp.pad(w_gate_b[j], ((0, 0), (0, LANES - ngate))).astype(bf16))
            q_p = _heads_prompt(qf, tp).astype(bf16)
            q_s = _heads_sample(qf, tp, ts, db).astype(bf16)

            tq = WINDOW
            iw = jnp.arange(tq)
            jj = jnp.arange(tq // BLK + 2)
            dist_c = iw[:, None] - (BLK * (jj[None, :] - 2) + BLK - 1)
            bc = _dist_bias(lut, dist_c, dist_c >= 0)
            bc = bc.transpose(0, 2, 1)[:, :, None, :]
            nqt = tp // tq
            qt_p = _heads_prompt(qf, tp).reshape(KV, G, nqt, tq, HEAD_DIM).transpose(0, 2, 4, 1, 3)
            qt_p = qt_p.reshape(KV, nqt, HEAD_DIM, G * tq).astype(bf16)
            oc_t, sel_t = _cmp_prompt(qt_p, kc_p, vc_p.transpose(0, 2, 1), bc,
                                      _head_rows(c31, tq).transpose(0, 2, 1), G, tq)
            oc_p = oc_t.reshape(KV, nqt, HEAD_DIM, G, tq).transpose(0, 3, 1, 4, 2).reshape(KV, G, tp, HEAD_DIM)
            sel_p = sel_t.transpose(0, 2, 1)
            dist_cs = (past + jnp.arange(ts))[:, None] - (BLK * jnp.arange(nbp)[None, :] + BLK - 1)
            bias_cs = _dist_bias(lut, dist_cs, dist_cs >= 0)
            cur_s = ((past + jnp.arange(ts)) // BLK).astype(i32).reshape(ts, 1)
            oc_s, sel_s = _cmp_sample(q_s, kc_s, vc_s, bias_cs, cur_s)

            tqs = SLC_TQ
            bias_sl = _toeplitz_bias(lut, tqs, 2 * tqs, tqs, sub=c31)
            kst_p = _kv_prompt(kvr[:, 2 * nk:3 * nk], tp).transpose(0, 2, 1).astype(bf16)
            vs_p = _kv_prompt(kvr[:, 3 * nk:4 * nk], tp)
            vs1_p = jnp.concatenate([vs_p, jnp.ones((KV, tp, 1), f32),
                                     jnp.zeros((KV, tp, LANES - HEAD_DIM - 1), f32)], axis=2).astype(bf16)
            os_p = _slc_prompt((_heads_prompt(qf, tp) * scale).astype(bf16), kst_p, vs1_p, sel_p,
                               _block_expand(sel_p.shape[-1], tp), bias_sl, tqs)

            rows = KV * G * ts
            eye = jnp.eye(KV, dtype=f32)
            qs_scaled = _heads_sample(qf, tp, ts, db) * scale
            qbd = (qs_scaled[:, :, :, None, :] * eye[None, :, None, :, None]).reshape(db, rows, nk).astype(bf16)
            selr = jnp.broadcast_to(sel_s[:, :, None], (db, KV, G, ts, nbp)).reshape(db, rows, nbp)
            ppt = min(SLC_PAGES_PER_TILE, npg)
            tk = ppt * psz
            bias_l = _toeplitz_bias(lut, ts, tk, tk, sub=c31).reshape(rows, tk)
            bias_n = _toeplitz_bias(lut, ts, BLK, 0, sub=c31).reshape(rows, BLK)
            new_t = jnp.pad(kv_s[:, :, 2 * nk:4 * nk], ((0, 0), (0, BLK - ts), (0, 0))).transpose(0, 2, 1)
            knew, vnew = new_t[:, :nk].astype(bf16), new_t[:, nk:].astype(bf16)
            os_s = _slc_sample(pool.reshape(pool.shape[0], 4, nk, psz), 2, page_table, qbd, selr,
                               _block_expand(nbp, past), knew, vnew, bias_l, bias_n,
                               _block_expand(nbp, BLK, past // BLK), ppt)
            os_s = os_s.reshape(db, KV, G * ts, HEAD_DIM)

            kw_p = _kv_prompt(kvr[:, 4 * nk:5 * nk], tp).astype(bf16)
            vw_p = _kv_prompt(kvr[:, 5 * nk:6 * nk], tp).astype(bf16)
            ow_p = _band_attn(q_p, kw_p, vw_p, bias_band)
            kw_s = kv_s[:, :, 4 * nk:5 * nk].reshape(db, ts, KV, HEAD_DIM)
            vw_s = kv_s[:, :, 5 * nk:6 * nk].reshape(db, ts, KV, HEAD_DIM)
            kk = jnp.concatenate([state_nsa_win[:, :, 0], kw_s], axis=1).transpose(0, 2, 1, 3)
            vv = jnp.concatenate([state_nsa_win[:, :, 1], vw_s], axis=1).transpose(0, 2, 1, 3)
            ow_s = _step_attn(q_s, kk.astype(bf16), vv.astype(bf16), bias_step)

            head_of_lane = jnp.arange(nq) // HEAD_DIM
            expand = (jnp.arange(LANES)[None, :, None]
                      == (jnp.arange(3)[:, None, None] * N_HEADS + head_of_lane[None, None, :])).astype(bf16)
            merged = _merge(glog, _unheads(oc_p, oc_s, tp, ts, db), _unheads(os_p, os_s, tp, ts, db),
                            _unheads(ow_p, ow_s, tp, ts, db), expand)
            x = _mm_res(merged, w_o_b[j].astype(bf16), x, mods, 2 * layer, tp)

        x = _moe_layer(x, norm_g[layer, 1].reshape(1, d), mods, 2 * layer + 1, tp, layer,
                       router_w[layer], router_b[layer], w_up, b_up, w_down, b_down)

    y = _rmsnorm(x, final_norm_g.reshape(1, d))
    y_prompt = y[:tp].reshape(1, tp, d)
    y_sample = y[tp:].reshape(ts, db, d).transpose(1, 0, 2)
    return (y_prompt, y_sample, jnp.stack(swa_p), jnp.stack(swa_s), nsa_rows_p, nsa_rows_s, nsa_win_p, nsa_win_s)
```
